```python
import math
import jax, jax.numpy as jnp
from jax import lax
import numpy as np

D_MODEL = 2048
BATCH = 2
SEQ = 8192
DEPTH = 4

N_MIXERS = 2
GRID_W = 64
NA_HEADS = 16
NA_HEAD_DIM = D_MODEL // NA_HEADS
WIN_H = 8
WIN_W = 16
ML_PROJ_FACTOR = 2
ML_INNER = ML_PROJ_FACTOR * D_MODEL
ML_HEADS = 4
ML_HEAD_DIM = ML_INNER // ML_HEADS
ML_QKV_BLOCK = 4
ML_CONV_K = 4
ML_CHUNK = 64
MLP_HIDDEN = 4 * D_MODEL
RMS_EPS = 1e-6
LN_EPS = 1e-5

kernel_name = 'hybrid_natten_mlstm_encoder'


def rms_norm(x, g):
    xf = x.astype(jnp.float32)
    y = xf * lax.rsqrt(jnp.mean(xf * xf, axis=-1, keepdims=True) + RMS_EPS)
    return (y * g.astype(jnp.float32)).astype(x.dtype)


def neighbourhood_attention(h, w_qkv, q_gain, k_gain, rel_bias, w_o):
    B, T, _ = h.shape
    rows = T // GRID_W
    kh = min(WIN_H, rows)
    q, k, v = jnp.split(h @ w_qkv, 3, axis=-1)
    shp = (B, rows, GRID_W, NA_HEADS, NA_HEAD_DIM)
    q = rms_norm(q.reshape(shp), q_gain)
    k = rms_norm(k.reshape(shp), k_gain)
    v = v.reshape(shp)
    col_start = np.clip(np.arange(GRID_W) - WIN_W // 2, 0, GRID_W - WIN_W)
    col_idx = col_start[:, None] + np.arange(WIN_W)[None, :]
    rel_col = col_idx - np.arange(GRID_W)[:, None] + (WIN_W - 1)
    col_bias = rel_bias[:, :, rel_col]
    scale = NA_HEAD_DIM ** -0.5

    def row_block(r):
        r0 = jnp.clip(r - kh // 2, 0, rows - kh)
        q_r = lax.dynamic_index_in_dim(q, r, axis=1, keepdims=False)
        k_r = lax.dynamic_slice_in_dim(k, r0, kh, axis=1)[:, :, col_idx]
        v_r = lax.dynamic_slice_in_dim(v, r0, kh, axis=1)[:, :, col_idx]
        rel_row = r0 + jnp.arange(kh) - r + (WIN_H - 1)
        bias = jnp.take(col_bias, rel_row, axis=1).transpose(0, 2, 1, 3)
        s = jnp.einsum('bqhd,biqjhd->bhqij', q_r, k_r).astype(jnp.float32) * scale
        s = s + bias[None].astype(jnp.float32)
        p = jax.nn.softmax(s.reshape(B, NA_HEADS, GRID_W, kh * WIN_W), axis=-1)
        p = p.reshape(s.shape).astype(v.dtype)
        return jnp.einsum('bhqij,biqjhd->bqhd', p, v_r)

    out = lax.map(row_block, jnp.arange(rows))
    out = out.transpose(1, 0, 2, 3, 4).reshape(B, T, D_MODEL)
    return out @ w_o


def headwise(x, w):
    B, T, I = x.shape
    y = jnp.einsum('btgi,gio->btgo', x.reshape(B, T, I // ML_QKV_BLOCK, ML_QKV_BLOCK), w)
    return y.reshape(B, T, I)


def mlstm_chunkwise(q, k, v, i_pre, f_pre):
    B, H, T, Dh = q.shape
    L = ML_CHUNK
    nc = T // L
    k = k * (Dh ** -0.5)

    def to_chunks(a):
        return jnp.moveaxis(a.reshape((B, H, nc, L) + a.shape[3:]), 2, 0)

    qc, kc, vc = to_chunks(q), to_chunks(k), to_chunks(v)
    ic = to_chunks(i_pre)
    bc = jnp.cumsum(to_chunks(jax.nn.log_sigmoid(f_pre)), axis=-1)
    gc = bc[..., -1]
    tril = jnp.tril(jnp.ones((L, L), dtype=bool))

    def step(carry, xs):
        C, n, m = carry
        qj, kj, vj, ij, bj, gj = xs
        log_d = jnp.where(tril, bj[..., :, None] - bj[..., None, :] + ij[..., None, :], -jnp.inf)
        m_inter = bj + m[..., None]
        m_row = jnp.maximum(m_inter, jnp.max(log_d, axis=-1))
        s = jnp.einsum('bhld,bhsd->bhls', qj, kj) * jnp.exp(log_d - m_row[..., None])
        w_inter = jnp.exp(m_inter - m_row)
        num = jnp.einsum('bhls,bhsd->bhld', s, vj) + w_inter[..., None] * jnp.einsum('bhld,bhde->bhle', qj, C)
        den = jnp.sum(s, axis=-1) + w_inter * jnp.einsum('bhld,bhd->bhl', qj, n)
        h = num / jnp.maximum(jnp.abs(den), jnp.exp(-m_row))[..., None]
        log_w = gj[..., None] - bj + ij
        m_new = jnp.maximum(gj + m, jnp.max(log_w, axis=-1))
        wk = kj * jnp.exp(log_w - m_new[..., None])[..., None]
        decay = jnp.exp(gj + m - m_new)
        C_new = decay[..., None, None] * C + jnp.einsum('bhld,bhle->bhde', wk, vj)
        n_new = decay[..., None] * n + jnp.sum(wk, axis=2)
        return (C_new, n_new, m_new), h

    init = (jnp.zeros((B, H, Dh, Dh), jnp.float32), jnp.zeros((B, H, Dh), jnp.float32),
            jnp.zeros((B, H), jnp.float32))
    _, hs = lax.scan(step, init, (qc, kc, vc, ic, bc, gc))
    return jnp.moveaxis(hs, 0, 2).reshape(B, H, T, Dh)


def mlstm_layer(h, w_up, conv_w, conv_b, w_q, w_k, w_v, w_ig, b_ig, w_fg, b_fg, out_norm, skip, w_down):
    B, T, _ = h.shape
    I, NH, Dh = ML_INNER, ML_HEADS, ML_HEAD_DIM
    x_m, z = jnp.split(h @ w_up, 2, axis=-1)
    x_c = lax.conv_general_dilated(x_m, conv_w[:, None, :], window_strides=(1,),
                                   padding=[((ML_CONV_K - 1) // 2, ML_CONV_K // 2)],
                                   dimension_numbers=('NWC', 'WIO', 'NWC'),
                                   feature_group_count=I) + conv_b
    x_c = jax.nn.silu(x_c)
    q, k, v = headwise(x_c, w_q), headwise(x_c, w_k), headwise(x_m, w_v)

    def gate(w, b):
        return (q @ w[:I] + k @ w[I:2 * I] + v @ w[2 * I:] + b).astype(jnp.float32)

    i_pre, f_pre = gate(w_ig, b_ig), gate(w_fg, b_fg)
    heads = lambda a: a.astype(jnp.float32).reshape(B, T, NH, Dh).transpose(0, 2, 1, 3)
    qh, kh, vh = heads(q), heads(k), heads(v)
    gh = lambda a: a.transpose(0, 2, 1)
    h_f = mlstm_chunkwise(qh, kh, vh, gh(i_pre[..., :NH]), gh(f_pre[..., :NH]))
    fl = lambda a: jnp.flip(a, axis=2)
    h_b = fl(mlstm_chunkwise(fl(qh), fl(kh), fl(vh), jnp.flip(gh(i_pre[..., NH:]), -1),
                             jnp.flip(gh(f_pre[..., NH:]), -1)))
    hc = h_f + h_b
    mu = jnp.mean(hc, axis=-1, keepdims=True)
    var = jnp.mean(jnp.square(hc - mu), axis=-1, keepdims=True)
    hc = (hc - mu) * lax.rsqrt(var + LN_EPS)
    hc = hc.transpose(0, 2, 1, 3).reshape(B, T, I) * out_norm.astype(jnp.float32)
    y = (hc.astype(h.dtype) + skip * x_c) * jax.nn.silu(z)
    return y @ w_down


def sq_relu_mlp(h, w1, w2):
    a = jax.nn.relu(h @ w1)
    return (a * a) @ w2


def setup_inputs(seed: int = 0) -> dict:
    key = jax.random.key(seed)
    ks = iter(jax.random.split(key, 32))
    nrm = lambda shape, s: jax.random.normal(next(ks), shape, jnp.float32) * s
    n_a = (DEPTH + N_MIXERS - 1) // N_MIXERS
    n_m = DEPTH // N_MIXERS
    D, I, NH = D_MODEL, ML_INNER, ML_HEADS
    f_bias = jnp.tile(jnp.linspace(3.0, 6.0, NH, dtype=jnp.float32), 2)
    return {
        'x': nrm((BATCH, SEQ, D), 1.0),
        'norm_mix': 1.0 + nrm((DEPTH, D), 0.02),
        'norm_mlp': 1.0 + nrm((DEPTH, D), 0.02),
        'na_w_qkv': nrm((n_a, D, 3 * D), D ** -0.5),
        'na_q_gain': 1.0 + nrm((n_a, NA_HEAD_DIM), 0.02),
        'na_k_gain': 1.0 + nrm((n_a, NA_HEAD_DIM), 0.02),
        'na_rel_bias': nrm((n_a, NA_HEADS, 2 * WIN_H - 1, 2 * WIN_W - 1), 0.1),
        'na_w_o': nrm((n_a, D, D), D ** -0.5),
        'ml_w_up': nrm((n_m, D, 2 * I), D ** -0.5),
        'ml_conv_w': nrm((n_m, ML_CONV_K, I), ML_CONV_K ** -0.5),
        'ml_conv_b': nrm((n_m, I), 0.02),
        'ml_w_q': nrm((n_m, I // ML_QKV_BLOCK, ML_QKV_BLOCK, ML_QKV_BLOCK), ML_QKV_BLOCK ** -0.5),
        'ml_w_k': nrm((n_m, I // ML_QKV_BLOCK, ML_QKV_BLOCK, ML_QKV_BLOCK), ML_QKV_BLOCK ** -0.5),
        'ml_w_v': nrm((n_m, I // ML_QKV_BLOCK, ML_QKV_BLOCK, ML_QKV_BLOCK), ML_QKV_BLOCK ** -0.5),
        'ml_w_ig': nrm((n_m, 3 * I, 2 * NH), (3 * I) ** -0.5),
        'ml_b_ig': nrm((n_m, 2 * NH), 0.1),
        'ml_w_fg': nrm((n_m, 3 * I, 2 * NH), (3 * I) ** -0.5),
        'ml_b_fg': f_bias[None, :] + nrm((n_m, 2 * NH), 0.1),
        'ml_out_norm': 1.0 + nrm((n_m, I), 0.02),
        'ml_skip': 1.0 + nrm((n_m, I), 0.02),
        'ml_w_down': nrm((n_m, I, D), I ** -0.5),
        'mlp_w1': nrm((DEPTH, D, MLP_HIDDEN), D ** -0.5),
        'mlp_w2': nrm((DEPTH, MLP_HIDDEN, D), MLP_HIDDEN ** -0.5),
    }


def reference(x, norm_mix, norm_mlp, na_w_qkv, na_q_gain, na_k_gain, na_rel_bias, na_w_o,
              ml_w_up, ml_conv_w, ml_conv_b, ml_w_q, ml_w_k, ml_w_v, ml_w_ig, ml_b_ig,
              ml_w_fg, ml_b_fg, ml_out_norm, ml_skip, ml_w_down, mlp_w1, mlp_w2):
    for layer in range(DEPTH):
        j = layer // N_MIXERS
        hn = rms_norm(x, norm_mix[layer])
        if layer % N_MIXERS == 0:
            x = x + neighbourhood_attention(hn, na_w_qkv[j], na_q_gain[j], na_k_gain[j],
                                            na_rel_bias[j], na_w_o[j])
        else:
            x = x + mlstm_layer(hn, ml_w_up[j], ml_conv_w[j], ml_conv_b[j], ml_w_q[j], ml_w_k[j],
                                ml_w_v[j], ml_w_ig[j], ml_b_ig[j], ml_w_fg[j], ml_b_fg[j],
                                ml_out_norm[j], ml_skip[j], ml_w_down[j])
        x = x + sq_relu_mlp(rms_norm(x, norm_mlp[layer]), mlp_w1[layer], mlp_w2[layer])
    return x
```

```python
import functools

import numpy as np
import jax
import jax.numpy as jnp
from jax import lax
from jax.experimental import pallas as pl
from jax.experimental.pallas import tpu as pltpu

GRID_W = 64
NA_HEADS = 16
WIN_H = 8
WIN_W = 16
ML_HEADS = 4
ML_QKV_BLOCK = 4
ML_CONV_K = 4
RMS_EPS = 1e-6
LN_EPS = 1e-5

LANES = 128
MXU_DIM = 256
VMEM_BYTES_V7X = 64 * 1024 * 1024
VMEM_COMPILER_RESERVE = 6 * 1024 * 1024

ML_CHUNK = 256
NA_ROWS_PER_TILE = 8
NA_HEADS_PER_TILE = 4
NA_KEY_ROWS = 2 * NA_ROWS_PER_TILE

F32 = jnp.float32
BF16 = jnp.bfloat16


def _nbytes(shape, dtype):
    return int(np.prod(shape)) * jnp.dtype(dtype).itemsize


def _params(n_grid_axes, block_bytes, scratch_bytes=0, temp_bytes=0):
    need = 2 * block_bytes + scratch_bytes + temp_bytes + VMEM_COMPILER_RESERVE
    limit = min(max(need, 16 * 1024 * 1024), VMEM_BYTES_V7X - 2 * 1024 * 1024)
    return pltpu.CompilerParams(dimension_semantics=("arbitrary",) * n_grid_axes,
                                vmem_limit_bytes=int(limit))


def _rms_scale(x):
    return lax.rsqrt(jnp.mean(x * x, axis=-1, keepdims=True) + RMS_EPS)


def _silu(x):
    return x * (1.0 / (1.0 + jnp.exp(-x)))


def _norm_matmul_kernel(*refs, qk_norm, blocks_per_region):
    if qk_norm:
        x_ref, g_ref, w_ref, hg_ref, o_ref, xn_ref = refs
    else:
        x_ref, g_ref, w_ref, o_ref, xn_ref = refs
    j = pl.program_id(1)

    @pl.when(j == 0)
    def _():
        x = x_ref[...]
        xn_ref[...] = (x * _rms_scale(x) * g_ref[...]).astype(BF16)

    y = jnp.dot(xn_ref[...], w_ref[...], preferred_element_type=F32)
    if qk_norm:
        is_v = (j // blocks_per_region) == 2
        gain = hg_ref[0]
        for c in range(y.shape[1] // LANES):
            blk = y[:, c * LANES:(c + 1) * LANES]
            sc = jnp.where(is_v, 1.0, _rms_scale(blk))
            o_ref[:, c * LANES:(c + 1) * LANES] = (blk * sc * gain).astype(o_ref.dtype)
    else:
        o_ref[...] = y.astype(o_ref.dtype)


def _norm_matmul(x, g, w, head_gains=None, *, tm=512, tn=1024):
    m, k = x.shape
    n = w.shape[1]
    tm = min(tm, m)
    qk_norm = head_gains is not None
    in_specs = [pl.BlockSpec((tm, k), lambda i, j: (i, 0)),
                pl.BlockSpec((1, k), lambda i, j: (0, 0)),
                pl.BlockSpec((k, tn), lambda i, j: (0, j))]
    args = [x, g.reshape(1, k), w]
    bpr = 1
    if qk_norm:
        bpr = (n // 3) // tn
        in_specs.append(pl.BlockSpec((1, 1, LANES), lambda i, j: (j // bpr, 0, 0)))
        args.append(head_gains)
    blk = _nbytes((tm, k), F32) + _nbytes((k, tn), BF16) + _nbytes((tm, tn), BF16)
    return pl.pallas_call(
        functools.partial(_norm_matmul_kernel, qk_norm=qk_norm, blocks_per_region=bpr),
        grid=(m // tm, n // tn),
        in_specs=in_specs,
        out_specs=pl.BlockSpec((tm, tn), lambda i, j: (i, j)),
        out_shape=jax.ShapeDtypeStruct((m, n), BF16),
        scratch_shapes=[pltpu.VMEM((tm, k), BF16)],
        compiler_params=_params(2, blk, _nbytes((tm, k), BF16), 2 * _nbytes((tm, tn), F32)),
        name="norm_matmul_qk" if qk_norm else "norm_matmul",
    )(*args)


def _matmul_res_kernel(a_ref, w_ref, r_ref, o_ref):
    o_ref[...] = r_ref[...] + jnp.dot(a_ref[...], w_ref[...], preferred_element_type=F32)


def _matmul_res(a, w, res, *, tm=512, tn=512):
    m, k = a.shape
    n = w.shape[1]
    tm = min(tm, m)
    blk = _nbytes((tm, k), BF16) + _nbytes((k, tn), BF16) + 2 * _nbytes((tm, tn), F32)
    return pl.pallas_call(
        _matmul_res_kernel,
        grid=(m // tm, n // tn),
        in_specs=[pl.BlockSpec((tm, k), lambda i, j: (i, 0)),
                  pl.BlockSpec((k, tn), lambda i, j: (0, j)),
                  pl.BlockSpec((tm, tn), lambda i, j: (i, j))],
        out_specs=pl.BlockSpec((tm, tn), lambda i, j: (i, j)),
        out_shape=jax.ShapeDtypeStruct((m, n), F32),
        compiler_params=_params(2, blk, 0, _nbytes((tm, tn), F32)),
        name="matmul_res",
    )(a, w, res)


def _mlp_kernel(x_ref, g_ref, w1_ref, w2_ref, o_ref, xn_ref):
    j = pl.program_id(1)

    @pl.when(j == 0)
    def _():
        x = x_ref[...]
        xn_ref[...] = (x * _rms_scale(x) * g_ref[...]).astype(BF16)
        o_ref[...] = x

    h = jnp.dot(xn_ref[...], w1_ref[...], preferred_element_type=F32)
    h = jnp.maximum(h, 0.0)
    o_ref[...] += jnp.dot((h * h).astype(BF16), w2_ref[...], preferred_element_type=F32)


def _mlp(x, g, w1, w2, *, tm=512, th=512):
    m, d = x.shape
    hid = w1.shape[1]
    tm = min(tm, m)
    blk = 2 * _nbytes((tm, d), F32) + 2 * _nbytes((d, th), BF16)
    return pl.pallas_call(
        _mlp_kernel,
        grid=(m // tm, hid // th),
        in_specs=[pl.BlockSpec((tm, d), lambda i, j: (i, 0)),
                  pl.BlockSpec((1, d), lambda i, j: (0, 0)),
                  pl.BlockSpec((d, th), lambda i, j: (0, j)),
                  pl.BlockSpec((th, d), lambda i, j: (j, 0))],
        out_specs=pl.BlockSpec((tm, d), lambda i, j: (i, 0)),
        out_shape=jax.ShapeDtypeStruct((m, d), F32),
        scratch_shapes=[pltpu.VMEM((tm, d), BF16)],
        compiler_params=_params(2, blk, _nbytes((tm, d), BF16),
                                2 * _nbytes((tm, th), F32) + _nbytes((tm, d), F32)),
        name="sq_relu_mlp",
    )(x, g.reshape(1, d), w1, w2)


def _na_kernel(q_ref, k0, k1, k2, k3, v0, v1, v2, v3, bias_ref, o_ref, kbuf, vbuf, *, rows):
    t = pl.program_id(2)
    half = kbuf.shape[0] // 4
    for n, (kr, vr) in enumerate(((k0, v0), (k1, v1), (k2, v2), (k3, v3))):
        kbuf[n * half:(n + 1) * half, :] = kr[...]
        vbuf[n * half:(n + 1) * half, :] = vr[...]
    n_heads = q_ref.shape[1] // LANES
    win_keys = WIN_H * GRID_W
    for rr in range(NA_ROWS_PER_TILE):
        r = t * NA_ROWS_PER_TILE + rr
        r0 = jnp.clip(r - WIN_H // 2, 0, rows - WIN_H)
        d = r - r0
        off = r0 - (t * NA_ROWS_PER_TILE - WIN_H // 2)
        start = pl.multiple_of(off * GRID_W, GRID_W)
        for hh in range(n_heads):
            cols = slice(hh * LANES, (hh + 1) * LANES)
            q = q_ref[rr * GRID_W:(rr + 1) * GRID_W, cols]
            k = kbuf[pl.ds(start, win_keys), cols]
            v = vbuf[pl.ds(start, win_keys), cols]
            s = lax.dot_general(q, k, (((1,), (1,)), ((), ())), preferred_element_type=F32)
            s = s + bias_ref[hh, d]
            p = jnp.exp(s - jnp.max(s, axis=-1, keepdims=True))
            l = jnp.sum(p, axis=-1, keepdims=True)
            o = jnp.dot(p.astype(BF16), v, preferred_element_type=F32)
            o_ref[rr * GRID_W:(rr + 1) * GRID_W, cols] = (o / l).astype(o_ref.dtype)


def _na_attention(qkv, bias_tab, batch, seq):
    m = qkv.shape[0]
    d = qkv.shape[1] // 3
    rows = seq // GRID_W
    hg = NA_HEADS_PER_TILE
    cw = hg * LANES
    ncb = d // cw
    tq = NA_ROWS_PER_TILE * GRID_W
    tk = tq // 2
    rt = rows // NA_ROWS_PER_TILE
    nkb = seq // tk

    def kv_spec(n, region):
        def imap(b, g, t):
            return (b * nkb + jnp.clip(2 * t - 1 + n, 0, nkb - 1), region * ncb + g)
        return pl.BlockSpec((tk, cw), imap)

    in_specs = [pl.BlockSpec((tq, cw), lambda b, g, t: (b * rt + t, g))]
    in_specs += [kv_spec(n, 1) for n in range(4)] + [kv_spec(n, 2) for n in range(4)]
    in_specs.append(pl.BlockSpec((hg, WIN_H, GRID_W, WIN_H * GRID_W), lambda b, g, t: (g, 0, 0, 0)))
    blk = 2 * _nbytes((tq, cw), BF16) + 8 * _nbytes((tk, cw), BF16) + _nbytes((hg, WIN_H, GRID_W, WIN_H * GRID_W), F32)
    scratch = 2 * _nbytes((NA_KEY_ROWS * GRID_W, cw), BF16)
    return pl.pallas_call(
        functools.partial(_na_kernel, rows=rows),
        grid=(batch, ncb, rt),
        in_specs=in_specs,
        out_specs=pl.BlockSpec((tq, cw), lambda b, g, t: (b * rt + t, g)),
        out_shape=jax.ShapeDtypeStruct((m, d), BF16),
        scratch_shapes=[pltpu.VMEM((NA_KEY_ROWS * GRID_W, cw), BF16),
                        pltpu.VMEM((NA_KEY_ROWS * GRID_W, cw), BF16)],
        compiler_params=_params(3, blk, scratch, 4 * 1024 * 1024),
        name="na_attention",
    )(qkv, *([qkv] * 8), bias_tab)


def _na_bias_table(rel_bias):
    d = np.arange(WIN_H)[:, None, None, None]
    c = np.arange(GRID_W)[None, :, None, None]
    i = np.arange(WIN_H)[None, None, :, None]
    kc = np.arange(GRID_W)[None, None, None, :]
    c0 = np.clip(c - WIN_W // 2, 0, GRID_W - WIN_W)
    valid = np.broadcast_to((kc >= c0) & (kc < c0 + WIN_W), (WIN_H, GRID_W, WIN_H, GRID_W))
    rel_row = np.broadcast_to(i - d + (WIN_H - 1), valid.shape)
    rel_col = np.broadcast_to(np.clip(kc - c + (WIN_W - 1), 0, 2 * WIN_W - 2), valid.shape)
    tab = rel_bias[:, rel_row, rel_col]
    tab = jnp.where(valid[None], tab, -jnp.inf)
    return tab.reshape(rel_bias.shape[0], WIN_H, GRID_W, WIN_H * GRID_W).astype(F32)


def _na_layer(x, g, w_qkv, q_gain, k_gain, rel_bias, w_o, batch, seq):
    head_dim = w_qkv.shape[0] // NA_HEADS
    gains = jnp.stack([q_gain.astype(F32) * (head_dim ** -0.5), k_gain.astype(F32),
                       jnp.ones_like(q_gain, F32)]).reshape(3, 1, head_dim)
    qkv = _norm_matmul(x, g, w_qkv.astype(BF16), gains)
    att = _na_attention(qkv, _na_bias_table(rel_bias), batch, seq)
    return _matmul_res(att, w_o.astype(BF16), x)


def _ml_pre_kernel(xm_ref, prev_ref, next_ref, cw_ref, cb_ref, wq_ref, wk_ref, wv_ref, wg_ref, gb_ref,
                   q_ref, k_ref, v_ref, xc_ref, g_ref, *, seq, k_scale):
    i = pl.program_id(0)
    j = pl.program_id(1)
    tm = xm_ref.shape[0]
    xm = xm_ref[...]
    x = xm.astype(F32)
    t0 = (i * tm) % seq
    prev = prev_ref[...].astype(F32)
    nxt = next_ref[...].astype(F32)
    prev_row = jnp.where(t0 == 0, 0.0, prev[prev.shape[0] - 1:, :])
    nxt = jnp.where(t0 + tm == seq, 0.0, nxt[0:2, :])
    row = lax.broadcasted_iota(jnp.int32, (tm, 1), 0)
    x_m1 = jnp.where(row == 0, prev_row, pltpu.roll(x, 1, axis=0))
    x_p1 = jnp.where(row == tm - 1, nxt[0:1], pltpu.roll(x, tm - 1, axis=0))
    x_p2 = jnp.where(row == tm - 2, nxt[0:1],
                     jnp.where(row == tm - 1, nxt[1:2], pltpu.roll(x, tm - 2, axis=0)))
    cw = cw_ref[...]
    xc = x_m1 * cw[0:1] + x * cw[1:2] + x_p1 * cw[2:3] + x_p2 * cw[3:4] + cb_ref[...]
    xc = _silu(xc)
    xcb = xc.astype(BF16)
    xc_ref[...] = xcb

    part = jnp.zeros(g_ref.shape, F32)
    for c in range(xm.shape[1] // MXU_DIM):
        sl = slice(c * MXU_DIM, (c + 1) * MXU_DIM)
        qc = jnp.dot(xcb[:, sl], wq_ref[c], preferred_element_type=F32).astype(BF16)
        kc = jnp.dot(xcb[:, sl], wk_ref[c], preferred_element_type=F32).astype(BF16)
        vc = jnp.dot(xm[:, sl], wv_ref[c], preferred_element_type=F32).astype(BF16)
        q_ref[:, sl] = qc
        k_ref[:, sl] = kc * k_scale
        v_ref[:, sl] = vc
        part += jnp.dot(qc, wg_ref[0, sl, :], preferred_element_type=F32)
        part += jnp.dot(kc, wg_ref[1, sl, :], preferred_element_type=F32)
        part += jnp.dot(vc, wg_ref[2, sl, :], preferred_element_type=F32)

    @pl.when(j == 0)
    def _():
        g_ref[...] = gb_ref[...] + part

    @pl.when(j > 0)
    def _():
        g_ref[...] += part


def _ml_pre(up, conv_w, conv_b, wq_bd, wk_bd, wv_bd, w_gates, b_gates, seq, *, tm=512, halo=16):
    m = up.shape[0]
    inner = up.shape[1] // 2
    tm = min(tm, seq)
    tf = inner // ML_HEADS
    nfb = inner // tf
    gpb = tf // MXU_DIM
    hb = tm // halo
    nhalo = m // halo
    head_dim = tf
    k_scale = head_dim ** -0.5
    assert 2.0 ** round(np.log2(k_scale)) == k_scale
    in_specs = [
        pl.BlockSpec((tm, tf), lambda i, j: (i, j)),
        pl.BlockSpec((halo, tf), lambda i, j: (jnp.maximum(i * hb - 1, 0), j)),
        pl.BlockSpec((halo, tf), lambda i, j: (jnp.minimum((i + 1) * hb, nhalo - 1), j)),
        pl.BlockSpec((ML_CONV_K, tf), lambda i, j: (0, j)),
        pl.BlockSpec((1, tf), lambda i, j: (0, j)),
        pl.BlockSpec((gpb, MXU_DIM, MXU_DIM), lambda i, j: (j, 0, 0)),
        pl.BlockSpec((gpb, MXU_DIM, MXU_DIM), lambda i, j: (j, 0, 0)),
        pl.BlockSpec((gpb, MXU_DIM, MXU_DIM), lambda i, j: (j, 0, 0)),
        pl.BlockSpec((3, tf, LANES), lambda i, j: (0, j, 0)),
        pl.BlockSpec((1, LANES), lambda i, j: (0, 0)),
    ]
    act = jax.ShapeDtypeStruct((m, inner), BF16)
    act_spec = pl.BlockSpec((tm, tf), lambda i, j: (i, j))
    blk = (5 * _nbytes((tm, tf), BF16) + 3 * _nbytes((gpb, MXU_DIM, MXU_DIM), BF16)
           + _nbytes((3, tf, LANES), BF16) + _nbytes((tm, LANES), F32))
    return pl.pallas_call(
        functools.partial(_ml_pre_kernel, seq=seq, k_scale=k_scale),
        grid=(m // tm, nfb),
        in_specs=in_specs,
        out_specs=[act_spec, act_spec, act_spec, act_spec, pl.BlockSpec((tm, LANES), lambda i, j: (i, 0))],
        out_shape=[act, act, act, act, jax.ShapeDtypeStruct((m, LANES), F32)],
        compiler_params=_params(2, blk, 0, 8 * _nbytes((tm, tf), F32)),
        name="mlstm_pre",
    )(up, up, up, conv_w, conv_b.reshape(1, inner), wq_bd, wk_bd, wv_bd, w_gates, b_gates)


def _split_dot(a, tri):
    hi = a.astype(BF16)
    r1 = a - hi.astype(F32)
    mid = r1.astype(BF16)
    lo = (r1 - mid.astype(F32)).astype(BF16)
    return (jnp.dot(hi, tri, preferred_element_type=F32) + jnp.dot(mid, tri, preferred_element_type=F32)
            + jnp.dot(lo, tri, preferred_element_type=F32))


def _gate_cumsum_kernel(f_ref, b_ref, *, n_chunks, n_dir_heads):
    f = f_ref[...]
    lf = jnp.minimum(f, 0.0) - jnp.log(1.0 + jnp.exp(-jnp.abs(f)))
    lc = f.shape[1]
    s = lax.broadcasted_iota(jnp.int32, (lc, lc), 0)
    l = lax.broadcasted_iota(jnp.int32, (lc, lc), 1)
    prefix = _split_dot(lf, (s <= l).astype(BF16))
    suffix = _split_dot(lf, (s >= l).astype(BF16))
    row = lax.broadcasted_iota(jnp.int32, (f.shape[0], 1), 0)
    backward = ((row // n_chunks) % (2 * n_dir_heads)) >= n_dir_heads
    b_ref[...] = jnp.where(backward, suffix, prefix)


def _gate_cumsum(f_rows, n_chunks):
    r, lc = f_rows.shape
    return pl.pallas_call(
        functools.partial(_gate_cumsum_kernel, n_chunks=n_chunks, n_dir_heads=ML_HEADS),
        grid=(1,),
        in_specs=[pl.BlockSpec((r, lc), lambda i: (0, 0))],
        out_specs=pl.BlockSpec((r, lc), lambda i: (0, 0)),
        out_shape=jax.ShapeDtypeStruct((r, lc), F32),
        compiler_params=_params(1, 2 * _nbytes((r, lc), F32), 0, 8 * _nbytes((r, lc), F32)),
        name="mlstm_gate_cumsum",
    )(f_rows)


def _ml_core_kernel(q_ref, k_ref, v_ref, ir_ref, br_ref, ic_ref, bc_ref, h_ref, c_ref, n_ref, m_ref, *, reverse):
    hh = pl.program_id(1)
    step = pl.program_id(2)
    lc = q_ref.shape[0]

    @pl.when(step == 0)
    def _():
        c_ref[...] = jnp.zeros_like(c_ref)
        n_ref[...] = jnp.zeros_like(n_ref)
        m_ref[...] = jnp.zeros_like(m_ref)

    gate_col = hh + (ML_HEADS if reverse else 0)
    pick = lax.broadcasted_iota(jnp.int32, ic_ref.shape, 1) == gate_col
    i_col = jnp.sum(jnp.where(pick, ic_ref[...], 0.0), axis=1, keepdims=True)
    b_col = jnp.sum(jnp.where(pick, bc_ref[...], 0.0), axis=1, keepdims=True)
    i_row = ir_ref[0]
    b_row = br_ref[0]
    g = b_row[:, 0:1] if reverse else b_row[:, lc - 1:lc]
    m = m_ref[...]

    l_idx = lax.broadcasted_iota(jnp.int32, (lc, lc), 0)
    s_idx = lax.broadcasted_iota(jnp.int32, (lc, lc), 1)
    visible = (s_idx >= l_idx) if reverse else (s_idx <= l_idx)
    log_d = jnp.where(visible, b_col - b_row + i_row, -jnp.inf)
    m_inter = b_col + m
    m_row = jnp.maximum(m_inter, jnp.max(log_d, axis=1, keepdims=True))
    q = q_ref[...]
    k = k_ref[...]
    v = v_ref[...]
    s = lax.dot_general(q, k, (((1,), (1,)), ((), ())), preferred_element_type=F32) * jnp.exp(log_d - m_row)
    w_inter = jnp.exp(m_inter - m_row)
    q_c = jnp.dot(q, c_ref[...].astype(BF16), preferred_element_type=F32)
    q_n = jnp.sum(q.astype(F32) * n_ref[...], axis=1, keepdims=True)
    num = jnp.dot(s.astype(BF16), v, preferred_element_type=F32) + w_inter * q_c
    den = jnp.sum(s, axis=1, keepdims=True) + w_inter * q_n
    h_ref[...] = (num / jnp.maximum(jnp.abs(den), jnp.exp(-m_row))).astype(h_ref.dtype)

    log_w = g - b_col + i_col
    m_new = jnp.maximum(g + m, jnp.max(log_w, axis=0, keepdims=True))
    wk = k.astype(F32) * jnp.exp(log_w - m_new)
    decay = jnp.exp(g + m - m_new)
    upd = lax.dot_general(wk.astype(BF16), v, (((0,), (0,)), ((), ())), preferred_element_type=F32)
    c_ref[...] = decay * c_ref[...] + upd
    n_ref[...] = decay * n_ref[...] + jnp.sum(wk, axis=0, keepdims=True)
    m_ref[...] = m_new


def _ml_core(q, k, v, i_rows, b_rows, i_cols, b_cols, batch, seq, *, reverse):
    m, inner = q.shape
    lc = ML_CHUNK
    nc = seq // lc
    dh = inner // ML_HEADS
    n_gate_cols = 2 * ML_HEADS

    def chunk(c):
        return nc - 1 - c if reverse else c

    def act_map(b, h, c):
        return (b * nc + chunk(c), h)

    def row_map(b, h, c):
        return ((b * n_gate_cols + h + (ML_HEADS if reverse else 0)) * nc + chunk(c), 0, 0)

    def col_map(b, h, c):
        return (b * nc + chunk(c), 0)

    act_spec = pl.BlockSpec((lc, dh), act_map)
    row_spec = pl.BlockSpec((1, 1, lc), row_map)
    col_spec = pl.BlockSpec((lc, n_gate_cols), col_map)
    blk = 4 * _nbytes((lc, dh), BF16) + 2 * _nbytes((8, lc), F32) + 2 * _nbytes((lc, LANES), F32)
    scratch = _nbytes((dh, dh), F32) + 2 * _nbytes((8, dh), F32)
    return pl.pallas_call(
        functools.partial(_ml_core_kernel, reverse=reverse),
        grid=(batch, ML_HEADS, nc),
        in_specs=[act_spec, act_spec, act_spec, row_spec, row_spec, col_spec, col_spec],
        out_specs=act_spec,
        out_shape=jax.ShapeDtypeStruct((m, inner), BF16),
        scratch_shapes=[pltpu.VMEM((dh, dh), F32), pltpu.VMEM((1, dh), F32), pltpu.VMEM((1, 1), F32)],
        compiler_params=_params(3, blk, scratch, 3 * _nbytes((dh, dh), F32)),
        name="mlstm_core_bwd" if reverse else "mlstm_core_fwd",
    )(q, k, v, i_rows, b_rows, i_cols, b_cols)


def _ml_combine_kernel(hf_ref, hb_ref, xc_ref, z_ref, on_ref, sk_ref, y_ref):
    hc = hf_ref[...].astype(F32) + hb_ref[...].astype(F32)
    mu = jnp.mean(hc, axis=-1, keepdims=True)
    cen = hc - mu
    var = jnp.mean(cen * cen, axis=-1, keepdims=True)
    hn = cen * lax.rsqrt(var + LN_EPS) * on_ref[...]
    y = (hn + sk_ref[...] * xc_ref[...].astype(F32)) * _silu(z_ref[...].astype(F32))
    y_ref[...] = y.astype(y_ref.dtype)


def _ml_combine(h_f, h_b, x_c, up, out_norm, skip, *, tm=512):
    m, inner = h_f.shape
    tm = min(tm, m)
    dh = inner // ML_HEADS
    act_spec = pl.BlockSpec((tm, dh), lambda i, j: (i, j))
    vec_spec = pl.BlockSpec((1, dh), lambda i, j: (0, j))
    return pl.pallas_call(
        _ml_combine_kernel,
        grid=(m // tm, ML_HEADS),
        in_specs=[act_spec, act_spec, act_spec, pl.BlockSpec((tm, dh), lambda i, j: (i, ML_HEADS + j)),
                  vec_spec, vec_spec],
        out_specs=act_spec,
        out_shape=jax.ShapeDtypeStruct((m, inner), BF16),
        compiler_params=_params(2, 5 * _nbytes((tm, dh), BF16), 0, 6 * _nbytes((tm, dh), F32)),
        name="mlstm_combine",
    )(h_f, h_b, x_c, up, out_norm.reshape(1, inner), skip.reshape(1, inner))


def _block_diag_tiles(w):
    per = MXU_DIM // ML_QKV_BLOCK
    w4 = w.reshape(-1, per, ML_QKV_BLOCK, ML_QKV_BLOCK)
    eye = jnp.eye(per, dtype=w.dtype)
    return jnp.einsum("ngio,gh->ngiho", w4, eye).reshape(-1, MXU_DIM, MXU_DIM).astype(BF16)


def _ml_layer(x, g, w_up, conv_w, conv_b, w_q, w_k, w_v, w_ig, b_ig, w_fg, b_fg, out_norm, skip, w_down,
              batch, seq):
    m = x.shape[0]
    inner = w_up.shape[1] // 2
    n_gate_cols = 2 * ML_HEADS
    nc = seq // ML_CHUNK
    up = _norm_matmul(x, g, w_up.astype(BF16))
    w_gates = jnp.concatenate([w_ig, w_fg], axis=1)
    w_gates = jnp.pad(w_gates, ((0, 0), (0, LANES - 2 * n_gate_cols))).reshape(3, inner, LANES).astype(BF16)
    b_gates = jnp.pad(jnp.concatenate([b_ig, b_fg]), (0, LANES - 2 * n_gate_cols)).reshape(1, LANES).astype(F32)
    q, k, v, x_c, gates = _ml_pre(up, conv_w.astype(F32), conv_b.astype(F32), _block_diag_tiles(w_q),
                                  _block_diag_tiles(w_k), _block_diag_tiles(w_v), w_gates, b_gates, seq)

    def to_rows(cols):
        return cols.reshape(batch, nc, ML_CHUNK, n_gate_cols).transpose(0, 3, 1, 2).reshape(-1, ML_CHUNK)

    def to_cols(rows):
        return rows.reshape(batch, n_gate_cols, nc, ML_CHUNK).transpose(0, 2, 3, 1).reshape(m, n_gate_cols)

    i_cols = gates[:, :n_gate_cols]
    i_rows = to_rows(i_cols)
    b_rows = _gate_cumsum(to_rows(gates[:, n_gate_cols:2 * n_gate_cols]), nc)
    b_cols = to_cols(b_rows)
    i_rows3 = i_rows.reshape(-1, 1, ML_CHUNK)
    b_rows3 = b_rows.reshape(-1, 1, ML_CHUNK)
    h_f = _ml_core(q, k, v, i_rows3, b_rows3, i_cols, b_cols, batch, seq, reverse=False)
    h_b = _ml_core(q, k, v, i_rows3, b_rows3, i_cols, b_cols, batch, seq, reverse=True)
    y = _ml_combine(h_f, h_b, x_c, up, out_norm.astype(F32), skip.astype(F32))
    return _matmul_res(y, w_down.astype(BF16), x)


def kernel(x, norm_mix, norm_mlp, na_w_qkv, na_q_gain, na_k_gain, na_rel_bias, na_w_o, ml_w_up, ml_conv_w, ml_conv_b, ml_w_q, ml_w_k, ml_w_v, ml_w_ig, ml_b_ig, ml_w_fg, ml_b_fg, ml_out_norm, ml_skip, ml_w_down, mlp_w1, mlp_w2):
    batch, seq, d = x.shape
    depth = norm_mix.shape[0]
    n_mixers = 2
    h = x.reshape(batch * seq, d).astype(F32)
    for layer in range(depth):
        j = layer // n_mixers
        if layer % n_mixers == 0:
            h = _na_layer(h, norm_mix[layer], na_w_qkv[j], na_q_gain[j], na_k_gain[j], na_rel_bias[j],
                          na_w_o[j], batch, seq)
        else:
            h = _ml_layer(h, norm_mix[layer], ml_w_up[j], ml_conv_w[j], ml_conv_b[j], ml_w_q[j], ml_w_k[j],
                          ml_w_v[j], ml_w_ig[j], ml_b_ig[j], ml_w_fg[j], ml_b_fg[j], ml_out_norm[j],
                          ml_skip[j], ml_w_down[j], batch, seq)
        h = _mlp(h, norm_mlp[layer], mlp_w1[layer].astype(BF16), mlp_w2[layer].astype(BF16))
    return h.reshape(batch, seq, d).astype(x.dtype)
```

```python
import functools

import numpy as np
import jax
import jax.numpy as jnp
from jax import lax
from jax.experimental import pallas as pl
from jax.experimental.pallas import tpu as pltpu

GRID_W = 64
NA_HEADS = 16
WIN_H = 8
WIN_W = 16
ML_HEADS = 4
ML_QKV_BLOCK = 4
ML_CONV_K = 4
RMS_EPS = 1e-6
LN_EPS = 1e-5

LANES = 128
MXU_DIM = 256
VMEM_BYTES_V7X = 64 * 1024 * 1024
VMEM_COMPILER_RESERVE = 6 * 1024 * 1024

ML_CHUNK = 256
NA_ROWS_PER_TILE = 8
NA_HEADS_PER_TILE = 4
NA_KEY_ROWS = 2 * NA_ROWS_PER_TILE

F32 = jnp.float32
BF16 = jnp.bfloat16


def _nbytes(shape, dtype):
    return int(np.prod(shape)) * jnp.dtype(dtype).itemsize


def _params(n_grid_axes, block_bytes, scratch_bytes=0, temp_bytes=0):
    need = 2 * block_bytes + scratch_bytes + temp_bytes + VMEM_COMPILER_RESERVE
    limit = min(max(need, 16 * 1024 * 1024), VMEM_BYTES_V7X - 2 * 1024 * 1024)
    return pltpu.CompilerParams(dimension_semantics=("arbitrary",) * n_grid_axes,
                                vmem_limit_bytes=int(limit))


def _rms_scale(x):
    return lax.rsqrt(jnp.mean(x * x, axis=-1, keepdims=True) + RMS_EPS)


def _silu(x):
    return x * (1.0 / (1.0 + jnp.exp(-x)))


def _norm_matmul_kernel(*refs, qk_norm, blocks_per_region):
    if qk_norm:
        x_ref, g_ref, w_ref, hg_ref, o_ref, xn_ref = refs
    else:
        x_ref, g_ref, w_ref, o_ref, xn_ref = refs
    j = pl.program_id(1)

    @pl.when(j == 0)
    def _():
        x = x_ref[...]
        xn_ref[...] = (x * _rms_scale(x) * g_ref[...]).astype(BF16)

    y = jnp.dot(xn_ref[...], w_ref[...], preferred_element_type=F32)
    if qk_norm:
        is_v = (j // blocks_per_region) == 2
        gain = hg_ref[0]
        for c in range(y.shape[1] // LANES):
            blk = y[:, c * LANES:(c + 1) * LANES]
            sc = jnp.where(is_v, 1.0, _rms_scale(blk))
            o_ref[:, c * LANES:(c + 1) * LANES] = (blk * sc * gain).astype(o_ref.dtype)
    else:
        o_ref[...] = y.astype(o_ref.dtype)


def _norm_matmul(x, g, w, head_gains=None, *, tm=512, tn=1024):
    m, k = x.shape
    n = w.shape[1]
    tm = min(tm, m)
    qk_norm = head_gains is not None
    in_specs = [pl.BlockSpec((tm, k), lambda i, j: (i, 0)),
                pl.BlockSpec((1, k), lambda i, j: (0, 0)),
                pl.BlockSpec((k, tn), lambda i, j: (0, j))]
    args = [x, g.reshape(1, k), w]
    bpr = 1
    if qk_norm:
        bpr = (n // 3) // tn
        in_specs.append(pl.BlockSpec((1, 1, LANES), lambda i, j: (j // bpr, 0, 0)))
        args.append(head_gains)
    blk = _nbytes((tm, k), F32) + _nbytes((k, tn), BF16) + _nbytes((tm, tn), BF16)
    return pl.pallas_call(
        functools.partial(_norm_matmul_kernel, qk_norm=qk_norm, blocks_per_region=bpr),
        grid=(m // tm, n // tn),
        in_specs=in_specs,
        out_specs=pl.BlockSpec((tm, tn), lambda i, j: (i, j)),
        out_shape=jax.ShapeDtypeStruct((m, n), BF16),
        scratch_shapes=[pltpu.VMEM((tm, k), BF16)],
        compiler_params=_params(2, blk, _nbytes((tm, k), BF16), 2 * _nbytes((tm, tn), F32)),
        name="norm_matmul_qk" if qk_norm else "norm_matmul",
    )(*args)


def _matmul_res_kernel(a_ref, w_ref, r_ref, o_ref):
    o_ref[...] = r_ref[...] + jnp.dot(a_ref[...], w_ref[...], preferred_element_type=F32)


def _matmul_res(a, w, res, *, tm=512, tn=512):
    m, k = a.shape
    n = w.shape[1]
    tm = min(tm, m)
    blk = _nbytes((tm, k), BF16) + _nbytes((k, tn), BF16) + 2 * _nbytes((tm, tn), F32)
    return pl.pallas_call(
        _matmul_res_kernel,
        grid=(m // tm, n // tn),
        in_specs=[pl.BlockSpec((tm, k), lambda i, j: (i, 0)),
                  pl.BlockSpec((k, tn), lambda i, j: (0, j)),
                  pl.BlockSpec((tm, tn), lambda i, j: (i, j))],
        out_specs=pl.BlockSpec((tm, tn), lambda i, j: (i, j)),
        out_shape=jax.ShapeDtypeStruct((m, n), F32),
        compiler_params=_params(2, blk, 0, _nbytes((tm, tn), F32)),
        name="matmul_res",
    )(a, w, res)


def _mlp_kernel(x_ref, g_ref, w1_ref, w2_ref, o_ref, xn_ref):
    j = pl.program_id(1)

    @pl.when(j == 0)
    def _():
        x = x_ref[...]
        xn_ref[...] = (x * _rms_scale(x) * g_ref[...]).astype(BF16)
        o_ref[...] = x

    h = jnp.dot(xn_ref[...], w1_ref[...], preferred_element_type=F32)
    h = jnp.maximum(h, 0.0)
    o_ref[...] += jnp.dot((h * h).astype(BF16), w2_ref[...], preferred_element_type=F32)


def _mlp(x, g, w1, w2, *, tm=512, th=512):
    m, d = x.shape
    hid = w1.shape[1]
    tm = min(tm, m)
    blk = 2 * _nbytes((tm, d), F32) + 2 * _nbytes((d, th), BF16)
    return pl.pallas_call(
        _mlp_kernel,
        grid=(m // tm, hid // th),
        in_specs=[pl.BlockSpec((tm, d), lambda i, j: (i, 0)),
                  pl.BlockSpec((1, d), lambda i, j: (0, 0)),
                  pl.BlockSpec((d, th), lambda i, j: (0, j)),
                  pl.BlockSpec((th, d), lambda i, j: (j, 0))],
        out_specs=pl.BlockSpec((tm, d), lambda i, j: (i, 0)),
        out_shape=jax.ShapeDtypeStruct((m, d), F32),
        scratch_shapes=[pltpu.VMEM((tm, d), BF16)],
        compiler_params=_params(2, blk, _nbytes((tm, d), BF16),
                                2 * _nbytes((tm, th), F32) + _nbytes((tm, d), F32)),
        name="sq_relu_mlp",
    )(x, g.reshape(1, d), w1, w2)


def _na_kernel(q_ref, k0, k1, k2, k3, v0, v1, v2, v3, bias_ref, o_ref, kbuf, vbuf, *, rows):
    t = pl.program_id(2)
    half = kbuf.shape[0] // 4
    for n, (kr, vr) in enumerate(((k0, v0), (k1, v1), (k2, v2), (k3, v3))):
        kbuf[n * half:(n + 1) * half, :] = kr[...]
        vbuf[n * half:(n + 1) * half, :] = vr[...]
    n_heads = q_ref.shape[1] // LANES
    win_keys = WIN_H * GRID_W
    for rr in range(NA_ROWS_PER_TILE):
        r = t * NA_ROWS_PER_TILE + rr
        r0 = jnp.clip(r - WIN_H // 2, 0, rows - WIN_H)
        d = r - r0
        off = r0 - (t * NA_ROWS_PER_TILE - WIN_H // 2)
        start = pl.multiple_of(off * GRID_W, GRID_W)
        for hh in range(n_heads):
            cols = slice(hh * LANES, (hh + 1) * LANES)
            q = q_ref[rr * GRID_W:(rr + 1) * GRID_W, cols]
            k = kbuf[pl.ds(start, win_keys), cols]
            v = vbuf[pl.ds(start, win_keys), cols]
            s = lax.dot_general(q, k, (((1,), (1,)), ((), ())), preferred_element_type=F32)
            s = s + bias_ref[hh, d]
            p = jnp.exp(s - jnp.max(s, axis=-1, keepdims=True))
            l = jnp.sum(p, axis=-1, keepdims=True)
            o = jnp.dot(p.astype(BF16), v, preferred_element_type=F32)
            o_ref[rr * GRID_W:(rr + 1) * GRID_W, cols] = (o / l).astype(o_ref.dtype)


def _na_attention(qkv, bias_tab, batch, seq):
    m = qkv.shape[0]
    d = qkv.shape[1] // 3
    rows = seq // GRID_W
    hg = NA_HEADS_PER_TILE
    cw = hg * LANES
    ncb = d // cw
    tq = NA_ROWS_PER_TILE * GRID_W
    tk = tq // 2
    rt = rows // NA_ROWS_PER_TILE
    nkb = seq // tk

    def kv_spec(n, region):
        def imap(b, g, t):
            return (b * nkb + jnp.clip(2 * t - 1 + n, 0, nkb - 1), region * ncb + g)
        return pl.BlockSpec((tk, cw), imap)

    in_specs = [pl.BlockSpec((tq, cw), lambda b, g, t: (b * rt + t, g))]
    in_specs += [kv_spec(n, 1) for n in range(4)] + [kv_spec(n, 2) for n in range(4)]
    in_specs.append(pl.BlockSpec((hg, WIN_H, GRID_W, WIN_H * GRID_W), lambda b, g, t: (g, 0, 0, 0)))
    blk = 2 * _nbytes((tq, cw), BF16) + 8 * _nbytes((tk, cw), BF16) + _nbytes((hg, WIN_H, GRID_W, WIN_H * GRID_W), F32)
    scratch = 2 * _nbytes((NA_KEY_ROWS * GRID_W, cw), BF16)
    return pl.pallas_call(
        functools.partial(_na_kernel, rows=rows),
        grid=(batch, ncb, rt),
        in_specs=in_specs,
        out_specs=pl.BlockSpec((tq, cw), lambda b, g, t: (b * rt + t, g)),
        out_shape=jax.ShapeDtypeStruct((m, d), BF16),
        scratch_shapes=[pltpu.VMEM((NA_KEY_ROWS * GRID_W, cw), BF16),
                        pltpu.VMEM((NA_KEY_ROWS * GRID_W, cw), BF16)],
        compiler_params=_params(3, blk, scratch, 4 * 1024 * 1024),
        name="na_attention",
    )(qkv, *([qkv] * 8), bias_tab)


def _na_bias_table(rel_bias):
    d = np.arange(WIN_H)[:, None]
    i = np.arange(WIN_H)[None, :]
    row_sel = (i - d + (WIN_H - 1))[:, :, None] == np.arange(2 * WIN_H - 1)
    c = np.arange(GRID_W)[:, None]
    kc = np.arange(GRID_W)[None, :]
    c0 = np.clip(c - WIN_W // 2, 0, GRID_W - WIN_W)
    valid = (kc >= c0) & (kc < c0 + WIN_W)
    col_sel = ((kc - c + (WIN_W - 1))[:, :, None] == np.arange(2 * WIN_W - 1)) & valid[:, :, None]
    tab = jnp.einsum("hrs,dir,cks->hdcik", rel_bias.astype(F32), row_sel.astype(np.float32),
                     col_sel.astype(np.float32), precision=lax.Precision.HIGHEST)
    tab = jnp.where(valid[None, None, :, None, :], tab, -jnp.inf)
    return tab.reshape(rel_bias.shape[0], WIN_H, GRID_W, WIN_H * GRID_W)


def _na_layer(x, g, w_qkv, q_gain, k_gain, rel_bias, w_o, batch, seq):
    head_dim = w_qkv.shape[0] // NA_HEADS
    gains = jnp.stack([q_gain.astype(F32) * (head_dim ** -0.5), k_gain.astype(F32),
                       jnp.ones_like(q_gain, F32)]).reshape(3, 1, head_dim)
    qkv = _norm_matmul(x, g, w_qkv.astype(BF16), gains)
    att = _na_attention(qkv, _na_bias_table(rel_bias), batch, seq)
    return _matmul_res(att, w_o.astype(BF16), x)


def _ml_pre_kernel(xm_ref, prev_ref, next_ref, cw_ref, cb_ref, wq_ref, wk_ref, wv_ref, wg_ref, gb_ref,
                   q_ref, k_ref, v_ref, xc_ref, g_ref, *, seq, k_scale):
    i = pl.program_id(0)
    j = pl.program_id(1)
    tm = xm_ref.shape[0]
    xm = xm_ref[...]
    x = xm.astype(F32)
    t0 = (i * tm) % seq
    prev = prev_ref[...].astype(F32)
    nxt = next_ref[...].astype(F32)
    prev_row = jnp.where(t0 == 0, 0.0, prev[prev.shape[0] - 1:, :])
    nxt = jnp.where(t0 + tm == seq, 0.0, nxt[0:2, :])
    row = lax.broadcasted_iota(jnp.int32, (tm, 1), 0)
    x_m1 = jnp.where(row == 0, prev_row, pltpu.roll(x, 1, axis=0))
    x_p1 = jnp.where(row == tm - 1, nxt[0:1], pltpu.roll(x, tm - 1, axis=0))
    x_p2 = jnp.where(row == tm - 2, nxt[0:1],
                     jnp.where(row == tm - 1, nxt[1:2], pltpu.roll(x, tm - 2, axis=0)))
    cw = cw_ref[...]
    xc = x_m1 * cw[0:1] + x * cw[1:2] + x_p1 * cw[2:3] + x_p2 * cw[3:4] + cb_ref[...]
    xc = _silu(xc)
    xcb = xc.astype(BF16)
    xc_ref[...] = xcb

    part = jnp.zeros(g_ref.shape, F32)
    for c in range(xm.shape[1] // MXU_DIM):
        sl = slice(c * MXU_DIM, (c + 1) * MXU_DIM)
        qc = jnp.dot(xcb[:, sl], wq_ref[c], preferred_element_type=F32).astype(BF16)
        kc = jnp.dot(xcb[:, sl], wk_ref[c], preferred_element_type=F32).astype(BF16)
        vc = jnp.dot(xm[:, sl], wv_ref[c], preferred_element_type=F32).astype(BF16)
        q_ref[:, sl] = qc
        k_ref[:, sl] = kc * k_scale
        v_ref[:, sl] = vc
        part += jnp.dot(qc, wg_ref[0, sl, :], preferred_element_type=F32)
        part += jnp.dot(kc, wg_ref[1, sl, :], preferred_element_type=F32)
        part += jnp.dot(vc, wg_ref[2, sl, :], preferred_element_type=F32)

    @pl.when(j == 0)
    def _():
        g_ref[...] = gb_ref[...] + part

    @pl.when(j > 0)
    def _():
        g_ref[...] += part


def _ml_pre(up, conv_w, conv_b, wq_bd, wk_bd, wv_bd, w_gates, b_gates, seq, *, tm=512, halo=16):
    m = up.shape[0]
    inner = up.shape[1] // 2
    tm = min(tm, seq)
    tf = inner // ML_HEADS
    nfb = inner // tf
    gpb = tf // MXU_DIM
    hb = tm // halo
    nhalo = m // halo
    head_dim = tf
    k_scale = head_dim ** -0.5
    assert 2.0 ** round(np.log2(k_scale)) == k_scale
    in_specs = [
        pl.BlockSpec((tm, tf), lambda i, j: (i, j)),
        pl.BlockSpec((halo, tf), lambda i, j: (jnp.maximum(i * hb - 1, 0), j)),
        pl.BlockSpec((halo, tf), lambda i, j: (jnp.minimum((i + 1) * hb, nhalo - 1), j)),
        pl.BlockSpec((ML_CONV_K, tf), lambda i, j: (0, j)),
        pl.BlockSpec((1, tf), lambda i, j: (0, j)),
        pl.BlockSpec((gpb, MXU_DIM, MXU_DIM), lambda i, j: (j, 0, 0)),
        pl.BlockSpec((gpb, MXU_DIM, MXU_DIM), lambda i, j: (j, 0, 0)),
        pl.BlockSpec((gpb, MXU_DIM, MXU_DIM), lambda i, j: (j, 0, 0)),
        pl.BlockSpec((3, tf, LANES), lambda i, j: (0, j, 0)),
        pl.BlockSpec((1, LANES), lambda i, j: (0, 0)),
    ]
    act = jax.ShapeDtypeStruct((m, inner), BF16)
    act_spec = pl.BlockSpec((tm, tf), lambda i, j: (i, j))
    blk = (5 * _nbytes((tm, tf), BF16) + 3 * _nbytes((gpb, MXU_DIM, MXU_DIM), BF16)
           + _nbytes((3, tf, LANES), BF16) + _nbytes((tm, LANES), F32))
    return pl.pallas_call(
        functools.partial(_ml_pre_kernel, seq=seq, k_scale=k_scale),
        grid=(m // tm, nfb),
        in_specs=in_specs,
        out_specs=[act_spec, act_spec, act_spec, act_spec, pl.BlockSpec((tm, LANES), lambda i, j: (i, 0))],
        out_shape=[act, act, act, act, jax.ShapeDtypeStruct((m, LANES), F32)],
        compiler_params=_params(2, blk, 0, 8 * _nbytes((tm, tf), F32)),
        name="mlstm_pre",
    )(up, up, up, conv_w, conv_b.reshape(1, inner), wq_bd, wk_bd, wv_bd, w_gates, b_gates)


def _split_dot(a, tri):
    hi = a.astype(BF16)
    r1 = a - hi.astype(F32)
    mid = r1.astype(BF16)
    lo = (r1 - mid.astype(F32)).astype(BF16)
    return (jnp.dot(hi, tri, preferred_element_type=F32) + jnp.dot(mid, tri, preferred_element_type=F32)
            + jnp.dot(lo, tri, preferred_element_type=F32))


def _gate_cumsum_kernel(f_ref, b_ref, *, n_chunks, n_dir_heads):
    f = f_ref[...]
    lf = jnp.minimum(f, 0.0) - jnp.log(1.0 + jnp.exp(-jnp.abs(f)))
    lc = f.shape[1]
    s = lax.broadcasted_iota(jnp.int32, (lc, lc), 0)
    l = lax.broadcasted_iota(jnp.int32, (lc, lc), 1)
    prefix = _split_dot(lf, (s <= l).astype(BF16))
    suffix = _split_dot(lf, (s >= l).astype(BF16))
    row = lax.broadcasted_iota(jnp.int32, (f.shape[0], 1), 0)
    backward = ((row // n_chunks) % (2 * n_dir_heads)) >= n_dir_heads
    b_ref[...] = jnp.where(backward, suffix, prefix)


def _gate_cumsum(f_rows, n_chunks):
    r, lc = f_rows.shape
    return pl.pallas_call(
        functools.partial(_gate_cumsum_kernel, n_chunks=n_chunks, n_dir_heads=ML_HEADS),
        grid=(1,),
        in_specs=[pl.BlockSpec((r, lc), lambda i: (0, 0))],
        out_specs=pl.BlockSpec((r, lc), lambda i: (0, 0)),
        out_shape=jax.ShapeDtypeStruct((r, lc), F32),
        compiler_params=_params(1, 2 * _nbytes((r, lc), F32), 0, 8 * _nbytes((r, lc), F32)),
        name="mlstm_gate_cumsum",
    )(f_rows)


def _ml_core_kernel(q_ref, k_ref, v_ref, ir_ref, br_ref, ic_ref, bc_ref, h_ref, c_ref, n_ref, m_ref, *, reverse):
    hh = pl.program_id(1)
    step = pl.program_id(2)
    lc = q_ref.shape[0]

    @pl.when(step == 0)
    def _():
        c_ref[...] = jnp.zeros_like(c_ref)
        n_ref[...] = jnp.zeros_like(n_ref)
        m_ref[...] = jnp.zeros_like(m_ref)

    gate_col = hh + (ML_HEADS if reverse else 0)
    pick = lax.broadcasted_iota(jnp.int32, ic_ref.shape, 1) == gate_col
    i_col = jnp.sum(jnp.where(pick, ic_ref[...], 0.0), axis=1, keepdims=True)
    b_col = jnp.sum(jnp.where(pick, bc_ref[...], 0.0), axis=1, keepdims=True)
    i_row = ir_ref[0]
    b_row = br_ref[0]
    g = b_row[:, 0:1] if reverse else b_row[:, lc - 1:lc]
    m = m_ref[...]

    l_idx = lax.broadcasted_iota(jnp.int32, (lc, lc), 0)
    s_idx = lax.broadcasted_iota(jnp.int32, (lc, lc), 1)
    visible = (s_idx >= l_idx) if reverse else (s_idx <= l_idx)
    log_d = jnp.where(visible, b_col - b_row + i_row, -jnp.inf)
    m_inter = b_col + m
    m_row = jnp.maximum(m_inter, jnp.max(log_d, axis=1, keepdims=True))
    q = q_ref[...]
    k = k_ref[...]
    v = v_ref[...]
    s = lax.dot_general(q, k, (((1,), (1,)), ((), ())), preferred_element_type=F32) * jnp.exp(log_d - m_row)
    w_inter = jnp.exp(m_inter - m_row)
    q_c = jnp.dot(q, c_ref[...].astype(BF16), preferred_element_type=F32)
    q_n = jnp.sum(q.astype(F32) * n_ref[...], axis=1, keepdims=True)
    num = jnp.dot(s.astype(BF16), v, preferred_element_type=F32) + w_inter * q_c
    den = jnp.sum(s, axis=1, keepdims=True) + w_inter * q_n
    h_ref[...] = (num / jnp.maximum(jnp.abs(den), jnp.exp(-m_row))).astype(h_ref.dtype)

    log_w = g - b_col + i_col
    m_new = jnp.maximum(g + m, jnp.max(log_w, axis=0, keepdims=True))
    wk = k.astype(F32) * jnp.exp(log_w - m_new)
    decay = jnp.exp(g + m - m_new)
    upd = lax.dot_general(wk.astype(BF16), v, (((0,), (0,)), ((), ())), preferred_element_type=F32)
    c_ref[...] = decay * c_ref[...] + upd
    n_ref[...] = decay * n_ref[...] + jnp.sum(wk, axis=0, keepdims=True)
    m_ref[...] = m_new


def _ml_core(q, k, v, i_rows, b_rows, i_cols, b_cols, batch, seq, *, reverse):
    m, inner = q.shape
    lc = ML_CHUNK
    nc = seq // lc
    dh = inner // ML_HEADS
    n_gate_cols = 2 * ML_HEADS

    def chunk(c):
        return nc - 1 - c if reverse else c

    def act_map(b, h, c):
        return (b * nc + chunk(c), h)

    def row_map(b, h, c):
        return ((b * n_gate_cols + h + (ML_HEADS if reverse else 0)) * nc + chunk(c), 0, 0)

    def col_map(b, h, c):
        return (b * nc + chunk(c), 0)

    act_spec = pl.BlockSpec((lc, dh), act_map)
    row_spec = pl.BlockSpec((1, 1, lc), row_map)
    col_spec = pl.BlockSpec((lc, n_gate_cols), col_map)
    blk = 4 * _nbytes((lc, dh), BF16) + 2 * _nbytes((8, lc), F32) + 2 * _nbytes((lc, LANES), F32)
    scratch = _nbytes((dh, dh), F32) + 2 * _nbytes((8, dh), F32)
    return pl.pallas_call(
        functools.partial(_ml_core_kernel, reverse=reverse),
        grid=(batch, ML_HEADS, nc),
        in_specs=[act_spec, act_spec, act_spec, row_spec, row_spec, col_spec, col_spec],
        out_specs=act_spec,
        out_shape=jax.ShapeDtypeStruct((m, inner), BF16),
        scratch_shapes=[pltpu.VMEM((dh, dh), F32), pltpu.VMEM((1, dh), F32), pltpu.VMEM((1, 1), F32)],
        compiler_params=_params(3, blk, scratch, 3 * _nbytes((dh, dh), F32)),
        name="mlstm_core_bwd" if reverse else "mlstm_core_fwd",
    )(q, k, v, i_rows, b_rows, i_cols, b_cols)


def _ml_combine_kernel(hf_ref, hb_ref, xc_ref, z_ref, on_ref, sk_ref, y_ref):
    hc = hf_ref[...].astype(F32) + hb_ref[...].astype(F32)
    mu = jnp.mean(hc, axis=-1, keepdims=True)
    cen = hc - mu
    var = jnp.mean(cen * cen, axis=-1, keepdims=True)
    hn = cen * lax.rsqrt(var + LN_EPS) * on_ref[...]
    y = (hn + sk_ref[...] * xc_ref[...].astype(F32)) * _silu(z_ref[...].astype(F32))
    y_ref[...] = y.astype(y_ref.dtype)


def _ml_combine(h_f, h_b, x_c, up, out_norm, skip, *, tm=512):
    m, inner = h_f.shape
    tm = min(tm, m)
    dh = inner // ML_HEADS
    act_spec = pl.BlockSpec((tm, dh), lambda i, j: (i, j))
    vec_spec = pl.BlockSpec((1, dh), lambda i, j: (0, j))
    return pl.pallas_call(
        _ml_combine_kernel,
        grid=(m // tm, ML_HEADS),
        in_specs=[act_spec, act_spec, act_spec, pl.BlockSpec((tm, dh), lambda i, j: (i, ML_HEADS + j)),
                  vec_spec, vec_spec],
        out_specs=act_spec,
        out_shape=jax.ShapeDtypeStruct((m, inner), BF16),
        compiler_params=_params(2, 5 * _nbytes((tm, dh), BF16), 0, 6 * _nbytes((tm, dh), F32)),
        name="mlstm_combine",
    )(h_f, h_b, x_c, up, out_norm.reshape(1, inner), skip.reshape(1, inner))


def _block_diag_tiles(w):
    per = MXU_DIM // ML_QKV_BLOCK
    w4 = w.reshape(-1, per, ML_QKV_BLOCK, ML_QKV_BLOCK)
    eye = jnp.eye(per, dtype=w.dtype)
    return jnp.einsum("ngio,gh->ngiho", w4, eye).reshape(-1, MXU_DIM, MXU_DIM).astype(BF16)


def _ml_layer(x, g, w_up, conv_w, conv_b, w_q, w_k, w_v, w_ig, b_ig, w_fg, b_fg, out_norm, skip, w_down,
              batch, seq):
    m = x.shape[0]
    inner = w_up.shape[1] // 2
    n_gate_cols = 2 * ML_HEADS
    nc = seq // ML_CHUNK
    up = _norm_matmul(x, g, w_up.astype(BF16))
    w_gates = jnp.concatenate([w_ig, w_fg], axis=1)
    w_gates = jnp.pad(w_gates, ((0, 0), (0, LANES - 2 * n_gate_cols))).reshape(3, inner, LANES).astype(BF16)
    b_gates = jnp.pad(jnp.concatenate([b_ig, b_fg]), (0, LANES - 2 * n_gate_cols)).reshape(1, LANES).astype(F32)
    q, k, v, x_c, gates = _ml_pre(up, conv_w.astype(F32), conv_b.astype(F32), _block_diag_tiles(w_q),
                                  _block_diag_tiles(w_k), _block_diag_tiles(w_v), w_gates, b_gates, seq)

    def to_rows(cols):
        return cols.reshape(batch, nc, ML_CHUNK, n_gate_cols).transpose(0, 3, 1, 2).reshape(-1, ML_CHUNK)

    def to_cols(rows):
        return rows.reshape(batch, n_gate_cols, nc, ML_CHUNK).transpose(0, 2, 3, 1).reshape(m, n_gate_cols)

    i_cols = gates[:, :n_gate_cols]
    i_rows = to_rows(i_cols)
    b_rows = _gate_cumsum(to_rows(gates[:, n_gate_cols:2 * n_gate_cols]), nc)
    b_cols = to_cols(b_rows)
    i_rows3 = i_rows.reshape(-1, 1, ML_CHUNK)
    b_rows3 = b_rows.reshape(-1, 1, ML_CHUNK)
    h_f = _ml_core(q, k, v, i_rows3, b_rows3, i_cols, b_cols, batch, seq, reverse=False)
    h_b = _ml_core(q, k, v, i_rows3, b_rows3, i_cols, b_cols, batch, seq, reverse=True)
    y = _ml_combine(h_f, h_b, x_c, up, out_norm.astype(F32), skip.astype(F32))
    return _matmul_res(y, w_down.astype(BF16), x)


def kernel(x, norm_mix, norm_mlp, na_w_qkv, na_q_gain, na_k_gain, na_rel_bias, na_w_o, ml_w_up, ml_conv_w, ml_conv_b, ml_w_q, ml_w_k, ml_w_v, ml_w_ig, ml_b_ig, ml_w_fg, ml_b_fg, ml_out_norm, ml_skip, ml_w_down, mlp_w1, mlp_w2):
    batch, seq, d = x.shape
    depth = norm_mix.shape[0]
    n_mixers = 2
    h = x.reshape(batch * seq, d).astype(F32)
    for layer in range(depth):
        j = layer // n_mixers
        if layer % n_mixers == 0:
            h = _na_layer(h, norm_mix[layer], na_w_qkv[j], na_q_gain[j], na_k_gain[j], na_rel_bias[j],
                          na_w_o[j], batch, seq)
        else:
            h = _ml_layer(h, norm_mix[layer], ml_w_up[j], ml_conv_w[j], ml_conv_b[j], ml_w_q[j], ml_w_k[j],
                          ml_w_v[j], ml_w_ig[j], ml_b_ig[j], ml_w_fg[j], ml_b_fg[j], ml_out_norm[j],
                          ml_skip[j], ml_w_down[j], batch, seq)
        h = _mlp(h, norm_mlp[layer], mlp_w1[layer].astype(BF16), mlp_w2[layer].astype(BF16))
    return h.reshape(batch, seq, d).astype(x.dtype)
```

```python
import functools

import numpy as np
import jax
import jax.numpy as jnp
from jax import lax
from jax.experimental import pallas as pl
from jax.experimental.pallas import tpu as pltpu

GRID_W = 64
NA_HEADS = 16
WIN_H = 8
WIN_W = 16
ML_HEADS = 4
ML_QKV_BLOCK = 4
ML_CONV_K = 4
RMS_EPS = 1e-6
LN_EPS = 1e-5

LANES = 128
MXU_DIM = 256
VMEM_BYTES_V7X = 64 * 1024 * 1024
VMEM_COMPILER_RESERVE = 6 * 1024 * 1024

ML_CHUNK = 512
NA_ROWS_PER_TILE = 8
NA_HEADS_PER_TILE = 4
NA_KEY_ROWS = 2 * NA_ROWS_PER_TILE

F32 = jnp.float32
BF16 = jnp.bfloat16


def _nbytes(shape, dtype):
    return int(np.prod(shape)) * jnp.dtype(dtype).itemsize


def _params(n_grid_axes, block_bytes, scratch_bytes=0, temp_bytes=0):
    need = 2 * block_bytes + scratch_bytes + temp_bytes + VMEM_COMPILER_RESERVE
    limit = min(max(need, 16 * 1024 * 1024), VMEM_BYTES_V7X - 2 * 1024 * 1024)
    return pltpu.CompilerParams(dimension_semantics=("arbitrary",) * n_grid_axes,
                                vmem_limit_bytes=int(limit))


def _rms_scale(x):
    return lax.rsqrt(jnp.mean(x * x, axis=-1, keepdims=True) + RMS_EPS)


def _silu(x):
    return x * (1.0 / (1.0 + jnp.exp(-x)))


def _norm_matmul_kernel(*refs, qk_norm, blocks_per_region):
    if qk_norm:
        x_ref, g_ref, w_ref, hg_ref, o_ref, xn_ref = refs
    else:
        x_ref, g_ref, w_ref, o_ref, xn_ref = refs
    j = pl.program_id(1)

    @pl.when(j == 0)
    def _():
        x = x_ref[...]
        xn_ref[...] = (x * _rms_scale(x) * g_ref[...]).astype(BF16)

    y = jnp.dot(xn_ref[...], w_ref[...], preferred_element_type=F32)
    if qk_norm:
        is_v = (j // blocks_per_region) == 2
        gain = hg_ref[0]
        for c in range(y.shape[1] // LANES):
            blk = y[:, c * LANES:(c + 1) * LANES]
            sc = jnp.where(is_v, 1.0, _rms_scale(blk))
            o_ref[:, c * LANES:(c + 1) * LANES] = (blk * sc * gain).astype(o_ref.dtype)
    else:
        o_ref[...] = y.astype(o_ref.dtype)


def _norm_matmul(x, g, w, layer, head_gains=None, *, tm=512, tn=2048):
    m, k = x.shape
    n = w.shape[2]
    tm = min(tm, m)
    qk_norm = head_gains is not None
    in_specs = [pl.BlockSpec((tm, k), lambda i, j: (i, 0)),
                pl.BlockSpec((1, k), lambda i, j: (0, 0)),
                pl.BlockSpec((None, k, tn), lambda i, j: (layer, 0, j))]
    args = [x, g.reshape(1, k), w]
    bpr = 1
    if qk_norm:
        bpr = (n // 3) // tn
        in_specs.append(pl.BlockSpec((1, 1, LANES), lambda i, j: (j // bpr, 0, 0)))
        args.append(head_gains)
    blk = _nbytes((tm, k), F32) + _nbytes((k, tn), BF16) + _nbytes((tm, tn), BF16)
    return pl.pallas_call(
        functools.partial(_norm_matmul_kernel, qk_norm=qk_norm, blocks_per_region=bpr),
        grid=(m // tm, n // tn),
        in_specs=in_specs,
        out_specs=pl.BlockSpec((tm, tn), lambda i, j: (i, j)),
        out_shape=jax.ShapeDtypeStruct((m, n), BF16),
        scratch_shapes=[pltpu.VMEM((tm, k), BF16)],
        compiler_params=_params(2, blk, _nbytes((tm, k), BF16), 2 * _nbytes((tm, tn), F32)),
        name="norm_matmul_qk" if qk_norm else "norm_matmul",
    )(*args)


def _matmul_res_kernel(a_ref, w_ref, r_ref, o_ref):
    o_ref[...] = r_ref[...] + jnp.dot(a_ref[...], w_ref[...], preferred_element_type=F32)


def _matmul_res(a, w, layer, res, *, tm=512, tn=1024):
    m, k = a.shape
    n = w.shape[2]
    tm = min(tm, m)
    blk = _nbytes((tm, k), BF16) + _nbytes((k, tn), BF16) + 2 * _nbytes((tm, tn), F32)
    return pl.pallas_call(
        _matmul_res_kernel,
        grid=(m // tm, n // tn),
        in_specs=[pl.BlockSpec((tm, k), lambda i, j: (i, 0)),
                  pl.BlockSpec((None, k, tn), lambda i, j: (layer, 0, j)),
                  pl.BlockSpec((tm, tn), lambda i, j: (i, j))],
        out_specs=pl.BlockSpec((tm, tn), lambda i, j: (i, j)),
        out_shape=jax.ShapeDtypeStruct((m, n), F32),
        compiler_params=_params(2, blk, 0, _nbytes((tm, tn), F32)),
        name="matmul_res",
    )(a, w, res)


def _mlp_kernel(x_ref, g_ref, w1_ref, w2_ref, o_ref, xn_ref):
    j = pl.program_id(1)

    @pl.when(j == 0)
    def _():
        x = x_ref[...]
        xn_ref[...] = (x * _rms_scale(x) * g_ref[...]).astype(BF16)
        o_ref[...] = x

    h = jnp.dot(xn_ref[...], w1_ref[...], preferred_element_type=F32)
    h = jnp.maximum(h, 0.0)
    o_ref[...] += jnp.dot((h * h).astype(BF16), w2_ref[...], preferred_element_type=F32)


def _mlp(x, g, w1, w2, layer, *, tm=512, th=1024):
    m, d = x.shape
    hid = w1.shape[2]
    tm = min(tm, m)
    blk = 2 * _nbytes((tm, d), F32) + 2 * _nbytes((d, th), BF16)
    return pl.pallas_call(
        _mlp_kernel,
        grid=(m // tm, hid // th),
        in_specs=[pl.BlockSpec((tm, d), lambda i, j: (i, 0)),
                  pl.BlockSpec((1, d), lambda i, j: (0, 0)),
                  pl.BlockSpec((None, d, th), lambda i, j: (layer, 0, j)),
                  pl.BlockSpec((None, th, d), lambda i, j: (layer, j, 0))],
        out_specs=pl.BlockSpec((tm, d), lambda i, j: (i, 0)),
        out_shape=jax.ShapeDtypeStruct((m, d), F32),
        scratch_shapes=[pltpu.VMEM((tm, d), BF16)],
        compiler_params=_params(2, blk, _nbytes((tm, d), BF16),
                                2 * _nbytes((tm, th), F32) + _nbytes((tm, d), F32)),
        name="sq_relu_mlp",
    )(x, g.reshape(1, d), w1, w2)


def _na_window_plan(tile_kind, rr):
    half = WIN_H // 2
    clamped = (tile_kind == 0 and rr < half) or (tile_kind == 2 and rr > half)
    if clamped:
        return half // 2, rr
    if rr % 2 == 0:
        return rr // 2, half
    return rr // 2, WIN_H


def _na_kernel(q_ref, k0, k1, k2, k3, v0, v1, v2, v3, bias_ref, o_ref, kbuf, vbuf, pbuf, *, n_row_tiles):
    t = pl.program_id(2)
    n_heads = q_ref.shape[1] // LANES
    quarter = kbuf.shape[0] // 4
    n_keys = kbuf.shape[0]
    win = bias_ref.shape[3]
    ones = jnp.ones((quarter, LANES), BF16)
    for n, (kr, vr) in enumerate(((k0, v0), (k1, v1), (k2, v2), (k3, v3))):
        rows_n = slice(n * quarter, (n + 1) * quarter)
        kbuf[rows_n, :] = kr[...]
        for hh in range(n_heads):
            vbuf[rows_n, 2 * hh * LANES:(2 * hh + 1) * LANES] = vr[:, hh * LANES:(hh + 1) * LANES]
            vbuf[rows_n, (2 * hh + 1) * LANES:(2 * hh + 2) * LANES] = ones

    def tile(tile_kind):
        for hh in range(n_heads):
            cols = slice(hh * LANES, (hh + 1) * LANES)
            s_all = lax.dot_general(q_ref[:, cols], kbuf[:, cols], (((1,), (1,)), ((), ())),
                                    preferred_element_type=F32)
            for rr in range(NA_ROWS_PER_TILE):
                blk0, slab = _na_window_plan(tile_kind, rr)
                lo = blk0 * LANES
                qrows = slice(rr * GRID_W, (rr + 1) * GRID_W)
                s = s_all[qrows, lo:lo + win] + bias_ref[hh, slab]
                p = jnp.exp(s - jnp.max(s, axis=-1, keepdims=True)).astype(BF16)
                if lo > 0:
                    pbuf[hh, qrows, 0:lo] = jnp.zeros((GRID_W, lo), BF16)
                pbuf[hh, qrows, lo:lo + win] = p
                if lo + win < n_keys:
                    pbuf[hh, qrows, lo + win:n_keys] = jnp.zeros((GRID_W, n_keys - lo - win), BF16)
            o = jnp.dot(pbuf[hh], vbuf[:, 2 * hh * LANES:(2 * hh + 2) * LANES], preferred_element_type=F32)
            o_ref[:, cols] = (o[:, :LANES] / o[:, LANES:]).astype(o_ref.dtype)

    @pl.when(t == 0)
    def _():
        tile(0)

    @pl.when(jnp.logical_and(t > 0, t < n_row_tiles - 1))
    def _():
        tile(1)

    @pl.when(t == n_row_tiles - 1)
    def _():
        tile(2)


def _na_attention(qkv, bias_tab, batch, seq):
    m = qkv.shape[0]
    d = qkv.shape[1] // 3
    rows = seq // GRID_W
    hg = NA_HEADS_PER_TILE
    cw = hg * LANES
    ncb = d // cw
    tq = NA_ROWS_PER_TILE * GRID_W
    tk = tq // 2
    rt = rows // NA_ROWS_PER_TILE
    assert rt >= 2
    nkb = seq // tk
    n_keys = NA_KEY_ROWS * GRID_W
    n_slabs, win = bias_tab.shape[1], bias_tab.shape[3]

    def kv_spec(n, region):
        def imap(b, g, t):
            return (b * nkb + jnp.clip(2 * t - 1 + n, 0, nkb - 1), region * ncb + g)
        return pl.BlockSpec((tk, cw), imap)

    in_specs = [pl.BlockSpec((tq, cw), lambda b, g, t: (b * rt + t, g))]
    in_specs += [kv_spec(n, 1) for n in range(4)] + [kv_spec(n, 2) for n in range(4)]
    in_specs.append(pl.BlockSpec((hg, n_slabs, GRID_W, win), lambda b, g, t: (g, 0, 0, 0)))
    blk = 2 * _nbytes((tq, cw), BF16) + 8 * _nbytes((tk, cw), BF16) + _nbytes((hg, n_slabs, GRID_W, win), F32)
    scratch_shapes = [pltpu.VMEM((n_keys, cw), BF16), pltpu.VMEM((n_keys, 2 * cw), BF16),
                      pltpu.VMEM((hg, tq, n_keys), BF16)]
    scratch = _nbytes((n_keys, 3 * cw), BF16) + _nbytes((hg, tq, n_keys), BF16)
    return pl.pallas_call(
        functools.partial(_na_kernel, n_row_tiles=rt),
        grid=(batch, ncb, rt),
        in_specs=in_specs,
        out_specs=pl.BlockSpec((tq, cw), lambda b, g, t: (b * rt + t, g)),
        out_shape=jax.ShapeDtypeStruct((m, d), BF16),
        scratch_shapes=scratch_shapes,
        compiler_params=_params(3, blk, scratch, 3 * _nbytes((tq, n_keys), F32)),
        name="na_attention",
    )(qkv, *([qkv] * 8), bias_tab)


def _na_bias_table(rel_bias):
    d = np.arange(WIN_H)[:, None]
    i = np.arange(WIN_H)[None, :]
    row_sel = (i - d + (WIN_H - 1))[:, :, None] == np.arange(2 * WIN_H - 1)
    c = np.arange(GRID_W)[:, None]
    kc = np.arange(GRID_W)[None, :]
    c0 = np.clip(c - WIN_W // 2, 0, GRID_W - WIN_W)
    valid = (kc >= c0) & (kc < c0 + WIN_W)
    col_sel = ((kc - c + (WIN_W - 1))[:, :, None] == np.arange(2 * WIN_W - 1)) & valid[:, :, None]
    tab = jnp.einsum("hrs,dir,cks->hdcik", rel_bias.astype(F32), row_sel.astype(np.float32),
                     col_sel.astype(np.float32), precision=lax.Precision.HIGHEST)
    tab = jnp.where(valid[None, None, :, None, :], tab, -jnp.inf)
    tab = tab.reshape(rel_bias.shape[0], WIN_H, GRID_W, WIN_H * GRID_W)

    def pad(a, lo, hi):
        return jnp.pad(a, ((0, 0), (0, 0), (0, 0), (lo, hi)), constant_values=-jnp.inf)

    shifted = pad(tab[:, WIN_H // 2:WIN_H // 2 + 1], LANES // 2, LANES // 2)
    return jnp.concatenate([pad(tab, 0, LANES), shifted], axis=1)


def _na_layer(x, g, w_qkv, j, q_gain, k_gain, rel_bias, w_o, batch, seq):
    head_dim = w_qkv.shape[1] // NA_HEADS
    gains = jnp.stack([q_gain.astype(F32) * (head_dim ** -0.5), k_gain.astype(F32),
                       jnp.ones_like(q_gain, F32)]).reshape(3, 1, head_dim)
    qkv = _norm_matmul(x, g, w_qkv, j, gains)
    att = _na_attention(qkv, _na_bias_table(rel_bias), batch, seq)
    return _matmul_res(att, w_o, j, x)


def _ml_pre_kernel(xm_ref, prev_ref, next_ref, cw_ref, cb_ref, wq_ref, wk_ref, wv_ref, wg_ref, gb_ref,
                   q_ref, k_ref, v_ref, xc_ref, g_ref, *, seq, k_scale):
    i = pl.program_id(0)
    j = pl.program_id(1)
    tm = xm_ref.shape[0]
    xm = xm_ref[...]
    x = xm.astype(F32)
    t0 = (i * tm) % seq
    prev = prev_ref[...].astype(F32)
    nxt = next_ref[...].astype(F32)
    prev_row = jnp.where(t0 == 0, 0.0, prev[prev.shape[0] - 1:, :])
    nxt = jnp.where(t0 + tm == seq, 0.0, nxt[0:2, :])
    row = lax.broadcasted_iota(jnp.int32, (tm, 1), 0)
    x_m1 = jnp.where(row == 0, prev_row, pltpu.roll(x, 1, axis=0))
    x_p1 = jnp.where(row == tm - 1, nxt[0:1], pltpu.roll(x, tm - 1, axis=0))
    x_p2 = jnp.where(row == tm - 2, nxt[0:1],
                     jnp.where(row == tm - 1, nxt[1:2], pltpu.roll(x, tm - 2, axis=0)))
    cw = cw_ref[...]
    xc = x_m1 * cw[0:1] + x * cw[1:2] + x_p1 * cw[2:3] + x_p2 * cw[3:4] + cb_ref[...]
    xc = _silu(xc)
    xcb = xc.astype(BF16)
    xc_ref[...] = xcb

    part = jnp.zeros(g_ref.shape, F32)
    for c in range(xm.shape[1] // MXU_DIM):
        sl = slice(c * MXU_DIM, (c + 1) * MXU_DIM)
        qc = jnp.dot(xcb[:, sl], wq_ref[c], preferred_element_type=F32).astype(BF16)
        kc = jnp.dot(xcb[:, sl], wk_ref[c], preferred_element_type=F32).astype(BF16)
        vc = jnp.dot(xm[:, sl], wv_ref[c], preferred_element_type=F32).astype(BF16)
        q_ref[:, sl] = qc
        k_ref[:, sl] = kc * k_scale
        v_ref[:, sl] = vc
        part += jnp.dot(qc, wg_ref[0, sl, :], preferred_element_type=F32)
        part += jnp.dot(kc, wg_ref[1, sl, :], preferred_element_type=F32)
        part += jnp.dot(vc, wg_ref[2, sl, :], preferred_element_type=F32)

    @pl.when(j == 0)
    def _():
        g_ref[...] = gb_ref[...] + part

    @pl.when(j > 0)
    def _():
        g_ref[...] += part


def _ml_pre(up, conv_w, conv_b, wq_bd, wk_bd, wv_bd, w_gates, b_gates, seq, *, tm=512, halo=16):
    m = up.shape[0]
    inner = up.shape[1] // 2
    tm = min(tm, seq)
    tf = inner // ML_HEADS
    nfb = inner // tf
    gpb = tf // MXU_DIM
    hb = tm // halo
    nhalo = m // halo
    head_dim = tf
    k_scale = head_dim ** -0.5
    assert 2.0 ** round(np.log2(k_scale)) == k_scale
    in_specs = [
        pl.BlockSpec((tm, tf), lambda i, j: (i, j)),
        pl.BlockSpec((halo, tf), lambda i, j: (jnp.maximum(i * hb - 1, 0), j)),
        pl.BlockSpec((halo, tf), lambda i, j: (jnp.minimum((i + 1) * hb, nhalo - 1), j)),
        pl.BlockSpec((ML_CONV_K, tf), lambda i, j: (0, j)),
        pl.BlockSpec((1, tf), lambda i, j: (0, j)),
        pl.BlockSpec((gpb, MXU_DIM, MXU_DIM), lambda i, j: (j, 0, 0)),
        pl.BlockSpec((gpb, MXU_DIM, MXU_DIM), lambda i, j: (j, 0, 0)),
        pl.BlockSpec((gpb, MXU_DIM, MXU_DIM), lambda i, j: (j, 0, 0)),
        pl.BlockSpec((3, tf, LANES), lambda i, j: (0, j, 0)),
        pl.BlockSpec((1, LANES), lambda i, j: (0, 0)),
    ]
    act = jax.ShapeDtypeStruct((m, inner), BF16)
    act_spec = pl.BlockSpec((tm, tf), lambda i, j: (i, j))
    blk = (5 * _nbytes((tm, tf), BF16) + 3 * _nbytes((gpb, MXU_DIM, MXU_DIM), BF16)
           + _nbytes((3, tf, LANES), BF16) + _nbytes((tm, LANES), F32))
    return pl.pallas_call(
        functools.partial(_ml_pre_kernel, seq=seq, k_scale=k_scale),
        grid=(m // tm, nfb),
        in_specs=in_specs,
        out_specs=[act_spec, act_spec, act_spec, act_spec, pl.BlockSpec((tm, LANES), lambda i, j: (i, 0))],
        out_shape=[act, act, act, act, jax.ShapeDtypeStruct((m, LANES), F32)],
        compiler_params=_params(2, blk, 0, 8 * _nbytes((tm, tf), F32)),
        name="mlstm_pre",
    )(up, up, up, conv_w, conv_b.reshape(1, inner), wq_bd, wk_bd, wv_bd, w_gates, b_gates)


def _split_dot(a, tri):
    hi = a.astype(BF16)
    r1 = a - hi.astype(F32)
    mid = r1.astype(BF16)
    lo = (r1 - mid.astype(F32)).astype(BF16)
    return (jnp.dot(hi, tri, preferred_element_type=F32) + jnp.dot(mid, tri, preferred_element_type=F32)
            + jnp.dot(lo, tri, preferred_element_type=F32))


def _gate_cumsum_kernel(f_ref, b_ref, *, n_chunks, n_dir_heads):
    f = f_ref[...]
    lf = jnp.minimum(f, 0.0) - jnp.log(1.0 + jnp.exp(-jnp.abs(f)))
    lc = f.shape[1]
    s = lax.broadcasted_iota(jnp.int32, (lc, lc), 0)
    l = lax.broadcasted_iota(jnp.int32, (lc, lc), 1)
    prefix = _split_dot(lf, (s <= l).astype(BF16))
    suffix = _split_dot(lf, (s >= l).astype(BF16))
    row = lax.broadcasted_iota(jnp.int32, (f.shape[0], 1), 0)
    backward = ((row // n_chunks) % (2 * n_dir_heads)) >= n_dir_heads
    b_ref[...] = jnp.where(backward, suffix, prefix)


def _gate_cumsum(f_rows, n_chunks):
    r, lc = f_rows.shape
    return pl.pallas_call(
        functools.partial(_gate_cumsum_kernel, n_chunks=n_chunks, n_dir_heads=ML_HEADS),
        grid=(1,),
        in_specs=[pl.BlockSpec((r, lc), lambda i: (0, 0))],
        out_specs=pl.BlockSpec((r, lc), lambda i: (0, 0)),
        out_shape=jax.ShapeDtypeStruct((r, lc), F32),
        compiler_params=_params(1, 2 * _nbytes((r, lc), F32), 0, 8 * _nbytes((r, lc), F32)),
        name="mlstm_gate_cumsum",
    )(f_rows)


def _ml_core_kernel(q_ref, k_ref, v_ref, ir_ref, br_ref, ic_ref, bc_ref, h_ref, c_ref, n_ref, m_ref, *, reverse):
    hh = pl.program_id(1)
    step = pl.program_id(2)
    lc = q_ref.shape[0]

    @pl.when(step == 0)
    def _():
        c_ref[...] = jnp.zeros_like(c_ref)
        n_ref[...] = jnp.zeros_like(n_ref)
        m_ref[...] = jnp.zeros_like(m_ref)

    gate_col = hh + (ML_HEADS if reverse else 0)
    pick = lax.broadcasted_iota(jnp.int32, ic_ref.shape, 1) == gate_col
    i_col = jnp.sum(jnp.where(pick, ic_ref[...], 0.0), axis=1, keepdims=True)
    b_col = jnp.sum(jnp.where(pick, bc_ref[...], 0.0), axis=1, keepdims=True)
    i_row = ir_ref[0]
    b_row = br_ref[0]
    g = b_row[:, 0:1] if reverse else b_row[:, lc - 1:lc]
    m = m_ref[...]

    l_idx = lax.broadcasted_iota(jnp.int32, (lc, lc), 0)
    s_idx = lax.broadcasted_iota(jnp.int32, (lc, lc), 1)
    visible = (s_idx >= l_idx) if reverse else (s_idx <= l_idx)
    log_d = jnp.where(visible, b_col - b_row + i_row, -jnp.inf)
    m_inter = b_col + m
    m_row = jnp.maximum(m_inter, jnp.max(log_d, axis=1, keepdims=True))
    q = q_ref[...]
    k = k_ref[...]
    v = v_ref[...]
    s = lax.dot_general(q, k, (((1,), (1,)), ((), ())), preferred_element_type=F32) * jnp.exp(log_d - m_row)
    w_inter = jnp.exp(m_inter - m_row)
    q_c = jnp.dot(q, c_ref[...].astype(BF16), preferred_element_type=F32)
    q_n = jnp.sum(q.astype(F32) * n_ref[...], axis=1, keepdims=True)
    num = jnp.dot(s.astype(BF16), v, preferred_element_type=F32) + w_inter * q_c
    den = jnp.sum(s, axis=1, keepdims=True) + w_inter * q_n
    h_ref[...] = (num / jnp.maximum(jnp.abs(den), jnp.exp(-m_row))).astype(h_ref.dtype)

    log_w = g - b_col + i_col
    m_new = jnp.maximum(g + m, jnp.max(log_w, axis=0, keepdims=True))
    wk = k.astype(F32) * jnp.exp(log_w - m_new)
    decay = jnp.exp(g + m - m_new)
    upd = lax.dot_general(wk.astype(BF16), v, (((0,), (0,)), ((), ())), preferred_element_type=F32)
    c_ref[...] = decay * c_ref[...] + upd
    n_ref[...] = decay * n_ref[...] + jnp.sum(wk, axis=0, keepdims=True)
    m_ref[...] = m_new


def _ml_core(q, k, v, i_rows, b_rows, i_cols, b_cols, batch, seq, *, reverse):
    m, inner = q.shape
    lc = ML_CHUNK
    nc = seq // lc
    dh = inner // ML_HEADS
    n_gate_cols = 2 * ML_HEADS

    def chunk(c):
        return nc - 1 - c if reverse else c

    def act_map(b, h, c):
        return (b * nc + chunk(c), h)

    def row_map(b, h, c):
        return ((b * n_gate_cols + h + (ML_HEADS if reverse else 0)) * nc + chunk(c), 0, 0)

    def col_map(b, h, c):
        return (b * nc + chunk(c), 0)

    act_spec = pl.BlockSpec((lc, dh), act_map)
    row_spec = pl.BlockSpec((1, 1, lc), row_map)
    col_spec = pl.BlockSpec((lc, LANES), col_map)
    blk = 4 * _nbytes((lc, dh), BF16) + 2 * _nbytes((8, lc), F32) + 2 * _nbytes((lc, LANES), F32)
    scratch = _nbytes((dh, dh), F32) + 2 * _nbytes((8, dh), F32)
    return pl.pallas_call(
        functools.partial(_ml_core_kernel, reverse=reverse),
        grid=(batch, ML_HEADS, nc),
        in_specs=[act_spec, act_spec, act_spec, row_spec, row_spec, col_spec, col_spec],
        out_specs=act_spec,
        out_shape=jax.ShapeDtypeStruct((m, inner), BF16),
        scratch_shapes=[pltpu.VMEM((dh, dh), F32), pltpu.VMEM((1, dh), F32), pltpu.VMEM((1, 1), F32)],
        compiler_params=_params(3, blk, scratch, 3 * _nbytes((dh, dh), F32)),
        name="mlstm_core_bwd" if reverse else "mlstm_core_fwd",
    )(q, k, v, i_rows, b_rows, i_cols, b_cols)


def _ml_combine_kernel(hf_ref, hb_ref, xc_ref, z_ref, on_ref, sk_ref, y_ref):
    hc = hf_ref[...].astype(F32) + hb_ref[...].astype(F32)
    mu = jnp.mean(hc, axis=-1, keepdims=True)
    cen = hc - mu
    var = jnp.mean(cen * cen, axis=-1, keepdims=True)
    hn = cen * lax.rsqrt(var + LN_EPS) * on_ref[...]
    y = (hn + sk_ref[...] * xc_ref[...].astype(F32)) * _silu(z_ref[...].astype(F32))
    y_ref[...] = y.astype(y_ref.dtype)


def _ml_combine(h_f, h_b, x_c, up, out_norm, skip, *, tm=512):
    m, inner = h_f.shape
    tm = min(tm, m)
    dh = inner // ML_HEADS
    act_spec = pl.BlockSpec((tm, dh), lambda i, j: (i, j))
    vec_spec = pl.BlockSpec((1, dh), lambda i, j: (0, j))
    return pl.pallas_call(
        _ml_combine_kernel,
        grid=(m // tm, ML_HEADS),
        in_specs=[act_spec, act_spec, act_spec, pl.BlockSpec((tm, dh), lambda i, j: (i, ML_HEADS + j)),
                  vec_spec, vec_spec],
        out_specs=act_spec,
        out_shape=jax.ShapeDtypeStruct((m, inner), BF16),
        compiler_params=_params(2, 5 * _nbytes((tm, dh), BF16), 0, 6 * _nbytes((tm, dh), F32)),
        name="mlstm_combine",
    )(h_f, h_b, x_c, up, out_norm.reshape(1, inner), skip.reshape(1, inner))


def _block_diag_tiles(w):
    per = MXU_DIM // ML_QKV_BLOCK
    w4 = w.reshape(-1, per, ML_QKV_BLOCK, ML_QKV_BLOCK)
    eye = jnp.eye(per, dtype=w.dtype)
    return jnp.einsum("ngio,gh->ngiho", w4, eye).reshape(-1, MXU_DIM, MXU_DIM).astype(BF16)


def _ml_layer(x, g, w_up, j, conv_w, conv_b, w_q, w_k, w_v, w_ig, b_ig, w_fg, b_fg, out_norm, skip, w_down,
              batch, seq):
    m = x.shape[0]
    inner = w_up.shape[2] // 2
    n_gate_cols = 2 * ML_HEADS
    nc = seq // ML_CHUNK
    up = _norm_matmul(x, g, w_up, j)
    w_gates = jnp.concatenate([w_ig, w_fg], axis=1)
    w_gates = jnp.pad(w_gates, ((0, 0), (0, LANES - 2 * n_gate_cols))).reshape(3, inner, LANES).astype(BF16)
    b_gates = jnp.pad(jnp.concatenate([b_ig, b_fg]), (0, LANES - 2 * n_gate_cols)).reshape(1, LANES).astype(F32)
    q, k, v, x_c, gates = _ml_pre(up, conv_w.astype(F32), conv_b.astype(F32), _block_diag_tiles(w_q),
                                  _block_diag_tiles(w_k), _block_diag_tiles(w_v), w_gates, b_gates, seq)

    def to_rows(cols):
        return cols.reshape(batch, nc, ML_CHUNK, n_gate_cols).transpose(0, 3, 1, 2).reshape(-1, ML_CHUNK)

    def to_cols(rows):
        cols = rows.reshape(batch, n_gate_cols, nc, ML_CHUNK).transpose(0, 2, 3, 1).reshape(m, n_gate_cols)
        return jnp.pad(cols, ((0, 0), (0, LANES - n_gate_cols)))

    i_rows = to_rows(gates[:, :n_gate_cols])
    b_rows = _gate_cumsum(to_rows(gates[:, n_gate_cols:2 * n_gate_cols]), nc)
    b_cols = to_cols(b_rows)
    i_rows3 = i_rows.reshape(-1, 1, ML_CHUNK)
    b_rows3 = b_rows.reshape(-1, 1, ML_CHUNK)
    h_f = _ml_core(q, k, v, i_rows3, b_rows3, gates, b_cols, batch, seq, reverse=False)
    h_b = _ml_core(q, k, v, i_rows3, b_rows3, gates, b_cols, batch, seq, reverse=True)
    y = _ml_combine(h_f, h_b, x_c, up, out_norm.astype(F32), skip.astype(F32))
    return _matmul_res(y, w_down, j, x)


def kernel(x, norm_mix, norm_mlp, na_w_qkv, na_q_gain, na_k_gain, na_rel_bias, na_w_o, ml_w_up, ml_conv_w, ml_conv_b, ml_w_q, ml_w_k, ml_w_v, ml_w_ig, ml_b_ig, ml_w_fg, ml_b_fg, ml_out_norm, ml_skip, ml_w_down, mlp_w1, mlp_w2):
    batch, seq, d = x.shape
    depth = norm_mix.shape[0]
    n_mixers = 2
    na_w_qkv, na_w_o, ml_w_up, ml_w_down, mlp_w1, mlp_w2 = (
        w.astype(BF16) for w in (na_w_qkv, na_w_o, ml_w_up, ml_w_down, mlp_w1, mlp_w2))
    h = x.reshape(batch * seq, d).astype(F32)
    for layer in range(depth):
        j = layer // n_mixers
        if layer % n_mixers == 0:
            h = _na_layer(h, norm_mix[layer], na_w_qkv, j, na_q_gain[j], na_k_gain[j], na_rel_bias[j],
                          na_w_o, batch, seq)
        else:
            h = _ml_layer(h, norm_mix[layer], ml_w_up, j, ml_conv_w[j], ml_conv_b[j], ml_w_q[j], ml_w_k[j],
                          ml_w_v[j], ml_w_ig[j], ml_b_ig[j], ml_w_fg[j], ml_b_fg[j], ml_out_norm[j],
                          ml_skip[j], ml_w_down, batch, seq)
        h = _mlp(h, norm_mlp[layer], mlp_w1, mlp_w2, layer)
    return h.reshape(batch, seq, d).astype(x.dtype)
```

```python
import functools

import numpy as np
import jax
import jax.numpy as jnp
from jax import lax
from jax.experimental import pallas as pl
from jax.experimental.pallas import tpu as pltpu

GRID_W = 64
NA_HEADS = 16
WIN_H = 8
WIN_W = 16
ML_HEADS = 4
ML_QKV_BLOCK = 4
ML_CONV_K = 4
RMS_EPS = 1e-6
LN_EPS = 1e-5

LANES = 128
MXU_DIM = 256
VMEM_BYTES_V7X = 64 * 1024 * 1024
VMEM_COMPILER_RESERVE = 6 * 1024 * 1024

ML_CHUNK = 512
NA_ROWS_PER_TILE = 8
NA_HEADS_PER_TILE = 4
NA_KEY_ROWS = 2 * NA_ROWS_PER_TILE

F32 = jnp.float32
BF16 = jnp.bfloat16


def _nbytes(shape, dtype):
    return int(np.prod(shape)) * jnp.dtype(dtype).itemsize


def _params(n_grid_axes, block_bytes, scratch_bytes=0, temp_bytes=0):
    need = 2 * block_bytes + scratch_bytes + temp_bytes + VMEM_COMPILER_RESERVE
    limit = min(max(need, 16 * 1024 * 1024), VMEM_BYTES_V7X - 2 * 1024 * 1024)
    return pltpu.CompilerParams(dimension_semantics=("arbitrary",) * n_grid_axes,
                                vmem_limit_bytes=int(limit))


def _rms_scale(x):
    return lax.rsqrt(jnp.mean(x * x, axis=-1, keepdims=True) + RMS_EPS)


def _silu(x):
    return x * (1.0 / (1.0 + jnp.exp(-x)))


def _norm_matmul_kernel(*refs, qk_norm, blocks_per_region):
    if qk_norm:
        x_ref, g_ref, w_ref, hg_ref, o_ref, xn_ref = refs
    else:
        x_ref, g_ref, w_ref, o_ref, xn_ref = refs
    j = pl.program_id(1)

    @pl.when(j == 0)
    def _():
        x = x_ref[...]
        xn_ref[...] = (x * _rms_scale(x) * g_ref[...]).astype(BF16)

    y = jnp.dot(xn_ref[...], w_ref[...], preferred_element_type=F32)
    if qk_norm:
        is_v = (j // blocks_per_region) == 2
        gain = hg_ref[0]
        for c in range(y.shape[1] // LANES):
            blk = y[:, c * LANES:(c + 1) * LANES]
            sc = jnp.where(is_v, 1.0, _rms_scale(blk))
            o_ref[:, c * LANES:(c + 1) * LANES] = (blk * sc * gain).astype(o_ref.dtype)
    else:
        o_ref[...] = y.astype(o_ref.dtype)


def _norm_matmul(x, g, w, layer, head_gains=None, *, tm=1024, tn=1024):
    m, k = x.shape
    n = w.shape[2]
    tm = min(tm, m)
    qk_norm = head_gains is not None
    in_specs = [pl.BlockSpec((tm, k), lambda i, j: (i, 0)),
                pl.BlockSpec((1, k), lambda i, j: (0, 0)),
                pl.BlockSpec((None, k, tn), lambda i, j: (layer, 0, j))]
    args = [x, g.reshape(1, k), w]
    bpr = 1
    if qk_norm:
        bpr = (n // 3) // tn
        in_specs.append(pl.BlockSpec((1, 1, LANES), lambda i, j: (j // bpr, 0, 0)))
        args.append(head_gains)
    blk = _nbytes((tm, k), F32) + _nbytes((k, tn), BF16) + _nbytes((tm, tn), BF16)
    return pl.pallas_call(
        functools.partial(_norm_matmul_kernel, qk_norm=qk_norm, blocks_per_region=bpr),
        grid=(m // tm, n // tn),
        in_specs=in_specs,
        out_specs=pl.BlockSpec((tm, tn), lambda i, j: (i, j)),
        out_shape=jax.ShapeDtypeStruct((m, n), BF16),
        scratch_shapes=[pltpu.VMEM((tm, k), BF16)],
        compiler_params=_params(2, blk, _nbytes((tm, k), BF16), 2 * _nbytes((tm, tn), F32)),
        name="norm_matmul_qk" if qk_norm else "norm_matmul",
    )(*args)


def _matmul_res_kernel(a_ref, w_ref, r_ref, o_ref):
    o_ref[...] = r_ref[...] + jnp.dot(a_ref[...], w_ref[...], preferred_element_type=F32)


def _matmul_res(a, w, layer, res, *, tm=512):
    m, k = a.shape
    n = w.shape[2]
    tm = min(tm, m)
    stream = _nbytes((tm, k), BF16) + 2 * _nbytes((tm, n), F32)
    return pl.pallas_call(
        _matmul_res_kernel,
        grid=(m // tm,),
        in_specs=[pl.BlockSpec((tm, k), lambda i: (i, 0)),
                  pl.BlockSpec((None, k, n), lambda i: (layer, 0, 0), pipeline_mode=pl.Buffered(1)),
                  pl.BlockSpec((tm, n), lambda i: (i, 0))],
        out_specs=pl.BlockSpec((tm, n), lambda i: (i, 0)),
        out_shape=jax.ShapeDtypeStruct((m, n), F32),
        compiler_params=_params(1, stream, _nbytes((k, n), BF16), _nbytes((tm, n), F32)),
        name="matmul_res",
    )(a, w, res)


def _mlp_kernel(x_ref, g_ref, w1_ref, w2_ref, o_ref, xn_ref):
    j = pl.program_id(1)

    @pl.when(j == 0)
    def _():
        x = x_ref[...]
        xn_ref[...] = (x * _rms_scale(x) * g_ref[...]).astype(BF16)
        o_ref[...] = x

    h = jnp.dot(xn_ref[...], w1_ref[...], preferred_element_type=F32)
    h = jnp.maximum(h, 0.0)
    o_ref[...] += jnp.dot((h * h).astype(BF16), w2_ref[...], preferred_element_type=F32)


def _mlp(x, g, w1, w2, layer, *, tm=512, th=1024):
    m, d = x.shape
    hid = w1.shape[2]
    tm = min(tm, m)
    blk = 2 * _nbytes((tm, d), F32) + 2 * _nbytes((d, th), BF16)
    return pl.pallas_call(
        _mlp_kernel,
        grid=(m // tm, hid // th),
        in_specs=[pl.BlockSpec((tm, d), lambda i, j: (i, 0)),
                  pl.BlockSpec((1, d), lambda i, j: (0, 0)),
                  pl.BlockSpec((None, d, th), lambda i, j: (layer, 0, j)),
                  pl.BlockSpec((None, th, d), lambda i, j: (layer, j, 0))],
        out_specs=pl.BlockSpec((tm, d), lambda i, j: (i, 0)),
        out_shape=jax.ShapeDtypeStruct((m, d), F32),
        scratch_shapes=[pltpu.VMEM((tm, d), BF16)],
        compiler_params=_params(2, blk, _nbytes((tm, d), BF16),
                                2 * _nbytes((tm, th), F32) + _nbytes((tm, d), F32)),
        name="sq_relu_mlp",
    )(x, g.reshape(1, d), w1, w2)


def _na_window_plan(tile_kind, rr):
    half = WIN_H // 2
    clamped = (tile_kind == 0 and rr < half) or (tile_kind == 2 and rr > half)
    if clamped:
        return half // 2, rr
    if rr % 2 == 0:
        return rr // 2, half
    return rr // 2, WIN_H


def _na_kernel(q_ref, k0, k1, k2, k3, v0, v1, v2, v3, bias_ref, o_ref, kbuf, vbuf, pbuf, *, n_row_tiles):
    t = pl.program_id(2)
    n_heads = q_ref.shape[1] // LANES
    quarter = kbuf.shape[0] // 4
    n_keys = kbuf.shape[0]
    win = bias_ref.shape[3]
    ones = jnp.ones((quarter, LANES), BF16)
    for n, (kr, vr) in enumerate(((k0, v0), (k1, v1), (k2, v2), (k3, v3))):
        rows_n = slice(n * quarter, (n + 1) * quarter)
        kbuf[rows_n, :] = kr[...]
        for hh in range(n_heads):
            vbuf[rows_n, 2 * hh * LANES:(2 * hh + 1) * LANES] = vr[:, hh * LANES:(hh + 1) * LANES]
            vbuf[rows_n, (2 * hh + 1) * LANES:(2 * hh + 2) * LANES] = ones

    def tile(tile_kind):
        for hh in range(n_heads):
            cols = slice(hh * LANES, (hh + 1) * LANES)
            s_all = lax.dot_general(q_ref[:, cols], kbuf[:, cols], (((1,), (1,)), ((), ())),
                                    preferred_element_type=F32)
            for rr in range(NA_ROWS_PER_TILE):
                blk0, slab = _na_window_plan(tile_kind, rr)
                lo = blk0 * LANES
                qrows = slice(rr * GRID_W, (rr + 1) * GRID_W)
                s = s_all[qrows, lo:lo + win] + bias_ref[hh, slab]
                p = jnp.exp(s - jnp.max(s, axis=-1, keepdims=True)).astype(BF16)
                if lo > 0:
                    pbuf[hh, qrows, 0:lo] = jnp.zeros((GRID_W, lo), BF16)
                pbuf[hh, qrows, lo:lo + win] = p
                if lo + win < n_keys:
                    pbuf[hh, qrows, lo + win:n_keys] = jnp.zeros((GRID_W, n_keys - lo - win), BF16)
            o = jnp.dot(pbuf[hh], vbuf[:, 2 * hh * LANES:(2 * hh + 2) * LANES], preferred_element_type=F32)
            o_ref[:, cols] = (o[:, :LANES] / o[:, LANES:]).astype(o_ref.dtype)

    @pl.when(t == 0)
    def _():
        tile(0)

    @pl.when(jnp.logical_and(t > 0, t < n_row_tiles - 1))
    def _():
        tile(1)

    @pl.when(t == n_row_tiles - 1)
    def _():
        tile(2)


def _na_attention(qkv, bias_tab, batch, seq):
    m = qkv.shape[0]
    d = qkv.shape[1] // 3
    rows = seq // GRID_W
    hg = NA_HEADS_PER_TILE
    cw = hg * LANES
    ncb = d // cw
    tq = NA_ROWS_PER_TILE * GRID_W
    tk = tq // 2
    rt = rows // NA_ROWS_PER_TILE
    assert rt >= 2
    nkb = seq // tk
    n_keys = NA_KEY_ROWS * GRID_W
    n_slabs, win = bias_tab.shape[1], bias_tab.shape[3]

    def kv_spec(n, region):
        def imap(b, g, t):
            return (b * nkb + jnp.clip(2 * t - 1 + n, 0, nkb - 1), region * ncb + g)
        return pl.BlockSpec((tk, cw), imap)

    in_specs = [pl.BlockSpec((tq, cw), lambda b, g, t: (b * rt + t, g))]
    in_specs += [kv_spec(n, 1) for n in range(4)] + [kv_spec(n, 2) for n in range(4)]
    in_specs.append(pl.BlockSpec((hg, n_slabs, GRID_W, win), lambda b, g, t: (g, 0, 0, 0)))
    blk = 2 * _nbytes((tq, cw), BF16) + 8 * _nbytes((tk, cw), BF16) + _nbytes((hg, n_slabs, GRID_W, win), F32)
    scratch_shapes = [pltpu.VMEM((n_keys, cw), BF16), pltpu.VMEM((n_keys, 2 * cw), BF16),
                      pltpu.VMEM((hg, tq, n_keys), BF16)]
    scratch = _nbytes((n_keys, 3 * cw), BF16) + _nbytes((hg, tq, n_keys), BF16)
    return pl.pallas_call(
        functools.partial(_na_kernel, n_row_tiles=rt),
        grid=(batch, ncb, rt),
        in_specs=in_specs,
        out_specs=pl.BlockSpec((tq, cw), lambda b, g, t: (b * rt + t, g)),
        out_shape=jax.ShapeDtypeStruct((m, d), BF16),
        scratch_shapes=scratch_shapes,
        compiler_params=_params(3, blk, scratch, 3 * _nbytes((tq, n_keys), F32)),
        name="na_attention",
    )(qkv, *([qkv] * 8), bias_tab)


def _na_bias_table(rel_bias):
    d = np.arange(WIN_H)[:, None]
    i = np.arange(WIN_H)[None, :]
    row_sel = (i - d + (WIN_H - 1))[:, :, None] == np.arange(2 * WIN_H - 1)
    c = np.arange(GRID_W)[:, None]
    kc = np.arange(GRID_W)[None, :]
    c0 = np.clip(c - WIN_W // 2, 0, GRID_W - WIN_W)
    valid = (kc >= c0) & (kc < c0 + WIN_W)
    col_sel = ((kc - c + (WIN_W - 1))[:, :, None] == np.arange(2 * WIN_W - 1)) & valid[:, :, None]
    tab = jnp.einsum("hrs,dir,cks->hdcik", rel_bias.astype(F32), row_sel.astype(np.float32),
                     col_sel.astype(np.float32), precision=lax.Precision.HIGHEST)
    tab = jnp.where(valid[None, None, :, None, :], tab, -jnp.inf)
    tab = tab.reshape(rel_bias.shape[0], WIN_H, GRID_W, WIN_H * GRID_W)

    def pad(a, lo, hi):
        return jnp.pad(a, ((0, 0), (0, 0), (0, 0), (lo, hi)), constant_values=-jnp.inf)

    shifted = pad(tab[:, WIN_H // 2:WIN_H // 2 + 1], LANES // 2, LANES // 2)
    return jnp.concatenate([pad(tab, 0, LANES), shifted], axis=1)


def _na_layer(x, g, w_qkv, j, q_gain, k_gain, rel_bias, w_o, batch, seq):
    head_dim = w_qkv.shape[1] // NA_HEADS
    gains = jnp.stack([q_gain.astype(F32) * (head_dim ** -0.5), k_gain.astype(F32),
                       jnp.ones_like(q_gain, F32)]).reshape(3, 1, head_dim)
    qkv = _norm_matmul(x, g, w_qkv, j, gains)
    att = _na_attention(qkv, _na_bias_table(rel_bias), batch, seq)
    return _matmul_res(att, w_o, j, x)


def _ml_pre_kernel(xm_ref, prev_ref, next_ref, cw_ref, cb_ref, wq_ref, wk_ref, wv_ref, wg_ref, gb_ref,
                   q_ref, k_ref, v_ref, xc_ref, g_ref, *, seq, k_scale):
    i = pl.program_id(0)
    j = pl.program_id(1)
    tm = xm_ref.shape[0]
    xm = xm_ref[...]
    x = xm.astype(F32)
    t0 = (i * tm) % seq
    prev = prev_ref[...].astype(F32)
    nxt = next_ref[...].astype(F32)
    prev_row = jnp.where(t0 == 0, 0.0, prev[prev.shape[0] - 1:, :])
    nxt = jnp.where(t0 + tm == seq, 0.0, nxt[0:2, :])
    row = lax.broadcasted_iota(jnp.int32, (tm, 1), 0)
    x_m1 = jnp.where(row == 0, prev_row, pltpu.roll(x, 1, axis=0))
    x_p1 = jnp.where(row == tm - 1, nxt[0:1], pltpu.roll(x, tm - 1, axis=0))
    x_p2 = jnp.where(row == tm - 2, nxt[0:1],
                     jnp.where(row == tm - 1, nxt[1:2], pltpu.roll(x, tm - 2, axis=0)))
    cw = cw_ref[...]
    xc = x_m1 * cw[0:1] + x * cw[1:2] + x_p1 * cw[2:3] + x_p2 * cw[3:4] + cb_ref[...]
    xc = _silu(xc)
    xcb = xc.astype(BF16)
    xc_ref[...] = xcb

    part = jnp.zeros(g_ref.shape, F32)
    for c in range(xm.shape[1] // MXU_DIM):
        sl = slice(c * MXU_DIM, (c + 1) * MXU_DIM)
        qc = jnp.dot(xcb[:, sl], wq_ref[c], preferred_element_type=F32).astype(BF16)
        kc = jnp.dot(xcb[:, sl], wk_ref[c], preferred_element_type=F32).astype(BF16)
        vc = jnp.dot(xm[:, sl], wv_ref[c], preferred_element_type=F32).astype(BF16)
        q_ref[:, sl] = qc
        k_ref[:, sl] = kc * k_scale
        v_ref[:, sl] = vc
        part += jnp.dot(qc, wg_ref[0, sl, :], preferred_element_type=F32)
        part += jnp.dot(kc, wg_ref[1, sl, :], preferred_element_type=F32)
        part += jnp.dot(vc, wg_ref[2, sl, :], preferred_element_type=F32)

    @pl.when(j == 0)
    def _():
        g_ref[...] = gb_ref[...] + part

    @pl.when(j > 0)
    def _():
        g_ref[...] += part


def _ml_pre(up, conv_w, conv_b, wq_bd, wk_bd, wv_bd, w_gates, b_gates, seq, *, tm=512, halo=16):
    m = up.shape[0]
    inner = up.shape[1] // 2
    tm = min(tm, seq)
    tf = inner // ML_HEADS
    nfb = inner // tf
    gpb = tf // MXU_DIM
    hb = tm // halo
    nhalo = m // halo
    head_dim = tf
    k_scale = head_dim ** -0.5
    assert 2.0 ** round(np.log2(k_scale)) == k_scale
    in_specs = [
        pl.BlockSpec((tm, tf), lambda i, j: (i, j)),
        pl.BlockSpec((halo, tf), lambda i, j: (jnp.maximum(i * hb - 1, 0), j)),
        pl.BlockSpec((halo, tf), lambda i, j: (jnp.minimum((i + 1) * hb, nhalo - 1), j)),
        pl.BlockSpec((ML_CONV_K, tf), lambda i, j: (0, j)),
        pl.BlockSpec((1, tf), lambda i, j: (0, j)),
        pl.BlockSpec((gpb, MXU_DIM, MXU_DIM), lambda i, j: (j, 0, 0)),
        pl.BlockSpec((gpb, MXU_DIM, MXU_DIM), lambda i, j: (j, 0, 0)),
        pl.BlockSpec((gpb, MXU_DIM, MXU_DIM), lambda i, j: (j, 0, 0)),
        pl.BlockSpec((3, tf, LANES), lambda i, j: (0, j, 0)),
        pl.BlockSpec((1, LANES), lambda i, j: (0, 0)),
    ]
    act = jax.ShapeDtypeStruct((m, inner), BF16)
    act_spec = pl.BlockSpec((tm, tf), lambda i, j: (i, j))
    blk = (5 * _nbytes((tm, tf), BF16) + 3 * _nbytes((gpb, MXU_DIM, MXU_DIM), BF16)
           + _nbytes((3, tf, LANES), BF16) + _nbytes((tm, LANES), F32))
    return pl.pallas_call(
        functools.partial(_ml_pre_kernel, seq=seq, k_scale=k_scale),
        grid=(m // tm, nfb),
        in_specs=in_specs,
        out_specs=[act_spec, act_spec, act_spec, act_spec, pl.BlockSpec((tm, LANES), lambda i, j: (i, 0))],
        out_shape=[act, act, act, act, jax.ShapeDtypeStruct((m, LANES), F32)],
        compiler_params=_params(2, blk, 0, 8 * _nbytes((tm, tf), F32)),
        name="mlstm_pre",
    )(up, up, up, conv_w, conv_b.reshape(1, inner), wq_bd, wk_bd, wv_bd, w_gates, b_gates)


def _split_dot(a, tri):
    hi = a.astype(BF16)
    r1 = a - hi.astype(F32)
    mid = r1.astype(BF16)
    lo = (r1 - mid.astype(F32)).astype(BF16)
    return (jnp.dot(hi, tri, preferred_element_type=F32) + jnp.dot(mid, tri, preferred_element_type=F32)
            + jnp.dot(lo, tri, preferred_element_type=F32))


def _gate_cumsum_kernel(f_ref, b_ref, *, n_chunks, n_dir_heads):
    f = f_ref[...]
    lf = jnp.minimum(f, 0.0) - jnp.log(1.0 + jnp.exp(-jnp.abs(f)))
    lc = f.shape[1]
    s = lax.broadcasted_iota(jnp.int32, (lc, lc), 0)
    l = lax.broadcasted_iota(jnp.int32, (lc, lc), 1)
    prefix = _split_dot(lf, (s <= l).astype(BF16))
    suffix = _split_dot(lf, (s >= l).astype(BF16))
    row = lax.broadcasted_iota(jnp.int32, (f.shape[0], 1), 0)
    backward = ((row // n_chunks) % (2 * n_dir_heads)) >= n_dir_heads
    b_ref[...] = jnp.where(backward, suffix, prefix)


def _gate_cumsum(f_rows, n_chunks):
    r, lc = f_rows.shape
    return pl.pallas_call(
        functools.partial(_gate_cumsum_kernel, n_chunks=n_chunks, n_dir_heads=ML_HEADS),
        grid=(1,),
        in_specs=[pl.BlockSpec((r, lc), lambda i: (0, 0))],
        out_specs=pl.BlockSpec((r, lc), lambda i: (0, 0)),
        out_shape=jax.ShapeDtypeStruct((r, lc), F32),
        compiler_params=_params(1, 2 * _nbytes((r, lc), F32), 0, 8 * _nbytes((r, lc), F32)),
        name="mlstm_gate_cumsum",
    )(f_rows)


def _ml_core_kernel(q_ref, k_ref, v_ref, ir_ref, br_ref, ic_ref, bc_ref, h_ref, c_ref, n_ref, m_ref, *, reverse):
    hh = pl.program_id(1)
    step = pl.program_id(2)
    lc = q_ref.shape[0]

    @pl.when(step == 0)
    def _():
        c_ref[...] = jnp.zeros_like(c_ref)
        n_ref[...] = jnp.zeros_like(n_ref)
        m_ref[...] = jnp.zeros_like(m_ref)

    gate_col = hh + (ML_HEADS if reverse else 0)
    pick = lax.broadcasted_iota(jnp.int32, ic_ref.shape, 1) == gate_col
    i_col = jnp.sum(jnp.where(pick, ic_ref[...], 0.0), axis=1, keepdims=True)
    b_col = jnp.sum(jnp.where(pick, bc_ref[...], 0.0), axis=1, keepdims=True)
    i_row = ir_ref[0]
    b_row = br_ref[0]
    g = b_row[:, 0:1] if reverse else b_row[:, lc - 1:lc]
    m = m_ref[...]

    l_idx = lax.broadcasted_iota(jnp.int32, (lc, lc), 0)
    s_idx = lax.broadcasted_iota(jnp.int32, (lc, lc), 1)
    visible = (s_idx >= l_idx) if reverse else (s_idx <= l_idx)
    log_d = jnp.where(visible, b_col - b_row + i_row, -jnp.inf)
    m_inter = b_col + m
    m_row = jnp.maximum(m_inter, jnp.max(log_d, axis=1, keepdims=True))
    q = q_ref[...]
    k = k_ref[...]
    v = v_ref[...]
    s = lax.dot_general(q, k, (((1,), (1,)), ((), ())), preferred_element_type=F32) * jnp.exp(log_d - m_row)
    w_inter = jnp.exp(m_inter - m_row)
    q_c = jnp.dot(q, c_ref[...].astype(BF16), preferred_element_type=F32)
    q_n = jnp.sum(q.astype(F32) * n_ref[...], axis=1, keepdims=True)
    num = jnp.dot(s.astype(BF16), v, preferred_element_type=F32) + w_inter * q_c
    den = jnp.sum(s, axis=1, keepdims=True) + w_inter * q_n
    h_ref[...] = (num / jnp.maximum(jnp.abs(den), jnp.exp(-m_row))).astype(h_ref.dtype)

    log_w = g - b_col + i_col
    m_new = jnp.maximum(g + m, jnp.max(log_w, axis=0, keepdims=True))
    wk = k.astype(F32) * jnp.exp(log_w - m_new)
    decay = jnp.exp(g + m - m_new)
    upd = lax.dot_general(wk.astype(BF16), v, (((0,), (0,)), ((), ())), preferred_element_type=F32)
    c_ref[...] = decay * c_ref[...] + upd
    n_ref[...] = decay * n_ref[...] + jnp.sum(wk, axis=0, keepdims=True)
    m_ref[...] = m_new


def _ml_core(q, k, v, i_rows, b_rows, i_cols, b_cols, batch, seq, *, reverse):
    m, inner = q.shape
    lc = ML_CHUNK
    nc = seq // lc
    dh = inner // ML_HEADS
    n_gate_cols = 2 * ML_HEADS

    def chunk(c):
        return nc - 1 - c if reverse else c

    def act_map(b, h, c):
        return (b * nc + chunk(c), h)

    def row_map(b, h, c):
        return ((b * n_gate_cols + h + (ML_HEADS if reverse else 0)) * nc + chunk(c), 0, 0)

    def col_map(b, h, c):
        return (b * nc + chunk(c), 0)

    act_spec = pl.BlockSpec((lc, dh), act_map)
    row_spec = pl.BlockSpec((1, 1, lc), row_map)
    col_spec = pl.BlockSpec((lc, LANES), col_map)
    blk = 4 * _nbytes((lc, dh), BF16) + 2 * _nbytes((8, lc), F32) + 2 * _nbytes((lc, LANES), F32)
    scratch = _nbytes((dh, dh), F32) + 2 * _nbytes((8, dh), F32)
    return pl.pallas_call(
        functools.partial(_ml_core_kernel, reverse=reverse),
        grid=(batch, ML_HEADS, nc),
        in_specs=[act_spec, act_spec, act_spec, row_spec, row_spec, col_spec, col_spec],
        out_specs=act_spec,
        out_shape=jax.ShapeDtypeStruct((m, inner), BF16),
        scratch_shapes=[pltpu.VMEM((dh, dh), F32), pltpu.VMEM((1, dh), F32), pltpu.VMEM((1, 1), F32)],
        compiler_params=_params(3, blk, scratch, 3 * _nbytes((dh, dh), F32)),
        name="mlstm_core_bwd" if reverse else "mlstm_core_fwd",
    )(q, k, v, i_rows, b_rows, i_cols, b_cols)


def _ml_combine_kernel(hf_ref, hb_ref, xc_ref, z_ref, on_ref, sk_ref, y_ref):
    hc = hf_ref[...].astype(F32) + hb_ref[...].astype(F32)
    mu = jnp.mean(hc, axis=-1, keepdims=True)
    cen = hc - mu
    var = jnp.mean(cen * cen, axis=-1, keepdims=True)
    hn = cen * lax.rsqrt(var + LN_EPS) * on_ref[...]
    y = (hn + sk_ref[...] * xc_ref[...].astype(F32)) * _silu(z_ref[...].astype(F32))
    y_ref[...] = y.astype(y_ref.dtype)


def _ml_combine(h_f, h_b, x_c, up, out_norm, skip, *, tm=512):
    m, inner = h_f.shape
    tm = min(tm, m)
    dh = inner // ML_HEADS
    act_spec = pl.BlockSpec((tm, dh), lambda i, j: (i, j))
    vec_spec = pl.BlockSpec((1, dh), lambda i, j: (0, j))
    return pl.pallas_call(
        _ml_combine_kernel,
        grid=(m // tm, ML_HEADS),
        in_specs=[act_spec, act_spec, act_spec, pl.BlockSpec((tm, dh), lambda i, j: (i, ML_HEADS + j)),
                  vec_spec, vec_spec],
        out_specs=act_spec,
        out_shape=jax.ShapeDtypeStruct((m, inner), BF16),
        compiler_params=_params(2, 5 * _nbytes((tm, dh), BF16), 0, 6 * _nbytes((tm, dh), F32)),
        name="mlstm_combine",
    )(h_f, h_b, x_c, up, out_norm.reshape(1, inner), skip.reshape(1, inner))


def _block_diag_tiles(w):
    per = MXU_DIM // ML_QKV_BLOCK
    w4 = w.reshape(-1, per, ML_QKV_BLOCK, ML_QKV_BLOCK)
    eye = jnp.eye(per, dtype=w.dtype)
    return jnp.einsum("ngio,gh->ngiho", w4, eye).reshape(-1, MXU_DIM, MXU_DIM).astype(BF16)


def _ml_layer(x, g, w_up, j, conv_w, conv_b, w_q, w_k, w_v, w_ig, b_ig, w_fg, b_fg, out_norm, skip, w_down,
              batch, seq):
    m = x.shape[0]
    inner = w_up.shape[2] // 2
    n_gate_cols = 2 * ML_HEADS
    nc = seq // ML_CHUNK
    up = _norm_matmul(x, g, w_up, j)
    w_gates = jnp.concatenate([w_ig, w_fg], axis=1)
    w_gates = jnp.pad(w_gates, ((0, 0), (0, LANES - 2 * n_gate_cols))).reshape(3, inner, LANES).astype(BF16)
    b_gates = jnp.pad(jnp.concatenate([b_ig, b_fg]), (0, LANES - 2 * n_gate_cols)).reshape(1, LANES).astype(F32)
    q, k, v, x_c, gates = _ml_pre(up, conv_w.astype(F32), conv_b.astype(F32), _block_diag_tiles(w_q),
                                  _block_diag_tiles(w_k), _block_diag_tiles(w_v), w_gates, b_gates, seq)

    def to_rows(cols):
        return cols.reshape(batch, nc, ML_CHUNK, n_gate_cols).transpose(0, 3, 1, 2).reshape(-1, ML_CHUNK)

    def to_cols(rows):
        cols = rows.reshape(batch, n_gate_cols, nc, ML_CHUNK).transpose(0, 2, 3, 1).reshape(m, n_gate_cols)
        return jnp.pad(cols, ((0, 0), (0, LANES - n_gate_cols)))

    i_rows = to_rows(gates[:, :n_gate_cols])
    b_rows = _gate_cumsum(to_rows(gates[:, n_gate_cols:2 * n_gate_cols]), nc)
    b_cols = to_cols(b_rows)
    i_rows3 = i_rows.reshape(-1, 1, ML_CHUNK)
    b_rows3 = b_rows.reshape(-1, 1, ML_CHUNK)
    h_f = _ml_core(q, k, v, i_rows3, b_rows3, gates, b_cols, batch, seq, reverse=False)
    h_b = _ml_core(q, k, v, i_rows3, b_rows3, gates, b_cols, batch, seq, reverse=True)
    y = _ml_combine(h_f, h_b, x_c, up, out_norm.astype(F32), skip.astype(F32))
    return _matmul_res(y, w_down, j, x)


def kernel(x, norm_mix, norm_mlp, na_w_qkv, na_q_gain, na_k_gain, na_rel_bias, na_w_o, ml_w_up, ml_conv_w, ml_conv_b, ml_w_q, ml_w_k, ml_w_v, ml_w_ig, ml_b_ig, ml_w_fg, ml_b_fg, ml_out_norm, ml_skip, ml_w_down, mlp_w1, mlp_w2):
    batch, seq, d = x.shape
    depth = norm_mix.shape[0]
    n_mixers = 2
    na_w_qkv, na_w_o, ml_w_up, ml_w_down, mlp_w1, mlp_w2 = (
        w.astype(BF16) for w in (na_w_qkv, na_w_o, ml_w_up, ml_w_down, mlp_w1, mlp_w2))
    h = x.reshape(batch * seq, d).astype(F32)
    for layer in range(depth):
        j = layer // n_mixers
        if layer % n_mixers == 0:
            h = _na_layer(h, norm_mix[layer], na_w_qkv, j, na_q_gain[j], na_k_gain[j], na_rel_bias[j],
                          na_w_o, batch, seq)
        else:
            h = _ml_layer(h, norm_mix[layer], ml_w_up, j, ml_conv_w[j], ml_conv_b[j], ml_w_q[j], ml_w_k[j],
                          ml_w_v[j], ml_w_ig[j], ml_b_ig[j], ml_w_fg[j], ml_b_fg[j], ml_out_norm[j],
                          ml_skip[j], ml_w_down, batch, seq)
        h = _mlp(h, norm_mlp[layer], mlp_w1, mlp_w2, layer)
    return h.reshape(batch, seq, d).astype(x.dtype)
```

```python
import functools

import numpy as np
import jax
import jax.numpy as jnp
from jax import lax
from jax.experimental import pallas as pl
from jax.experimental.pallas import tpu as pltpu

GRID_W = 64
NA_HEADS = 16
WIN_H = 8
WIN_W = 16
ML_HEADS = 4
ML_QKV_BLOCK = 4
ML_CONV_K = 4
RMS_EPS = 1e-6
LN_EPS = 1e-5

LANES = 128
MXU_DIM = 256
VMEM_BYTES_V7X = 64 * 1024 * 1024
VMEM_COMPILER_RESERVE = 6 * 1024 * 1024

ML_CHUNK = 512
NA_ROWS_PER_TILE = 8
NA_HEADS_PER_TILE = 4
NA_KEY_ROWS = 2 * NA_ROWS_PER_TILE

F32 = jnp.float32
BF16 = jnp.bfloat16


def _nbytes(shape, dtype):
    return int(np.prod(shape)) * jnp.dtype(dtype).itemsize


def _params(n_grid_axes, block_bytes, scratch_bytes=0, temp_bytes=0):
    need = 2 * block_bytes + scratch_bytes + temp_bytes + VMEM_COMPILER_RESERVE
    limit = min(max(need, 16 * 1024 * 1024), VMEM_BYTES_V7X - 2 * 1024 * 1024)
    return pltpu.CompilerParams(dimension_semantics=("arbitrary",) * n_grid_axes,
                                vmem_limit_bytes=int(limit))


def _rms_scale(x):
    return lax.rsqrt(jnp.mean(x * x, axis=-1, keepdims=True) + RMS_EPS)


def _silu(x):
    return x * (1.0 / (1.0 + jnp.exp(-x)))


def _norm_matmul_kernel(*refs, qk_norm, blocks_per_region):
    if qk_norm:
        x_ref, g_ref, w_ref, hg_ref, o_ref, xn_ref = refs
    else:
        x_ref, g_ref, w_ref, o_ref, xn_ref = refs
    j = pl.program_id(1)

    @pl.when(j == 0)
    def _():
        x = x_ref[...]
        xn_ref[...] = (x * _rms_scale(x) * g_ref[...]).astype(BF16)

    y = jnp.dot(xn_ref[...], w_ref[...], preferred_element_type=F32)
    if qk_norm:
        is_v = (j // blocks_per_region) == 2
        gain = hg_ref[0]
        for c in range(y.shape[1] // LANES):
            blk = y[:, c * LANES:(c + 1) * LANES]
            sc = jnp.where(is_v, 1.0, _rms_scale(blk))
            o_ref[:, c * LANES:(c + 1) * LANES] = (blk * sc * gain).astype(o_ref.dtype)
    else:
        o_ref[...] = y.astype(o_ref.dtype)


def _norm_matmul(x, g, w, layer, head_gains=None, *, tm=1024, tn=1024):
    m, k = x.shape
    n = w.shape[2]
    tm = min(tm, m)
    qk_norm = head_gains is not None
    in_specs = [pl.BlockSpec((tm, k), lambda i, j: (i, 0)),
                pl.BlockSpec((1, k), lambda i, j: (0, 0)),
                pl.BlockSpec((None, k, tn), lambda i, j: (layer, 0, j))]
    args = [x, g.reshape(1, k), w]
    bpr = 1
    if qk_norm:
        bpr = (n // 3) // tn
        in_specs.append(pl.BlockSpec((1, 1, LANES), lambda i, j: (j // bpr, 0, 0)))
        args.append(head_gains)
    blk = _nbytes((tm, k), F32) + _nbytes((k, tn), BF16) + _nbytes((tm, tn), BF16)
    return pl.pallas_call(
        functools.partial(_norm_matmul_kernel, qk_norm=qk_norm, blocks_per_region=bpr),
        grid=(m // tm, n // tn),
        in_specs=in_specs,
        out_specs=pl.BlockSpec((tm, tn), lambda i, j: (i, j)),
        out_shape=jax.ShapeDtypeStruct((m, n), BF16),
        scratch_shapes=[pltpu.VMEM((tm, k), BF16)],
        compiler_params=_params(2, blk, _nbytes((tm, k), BF16), 2 * _nbytes((tm, tn), F32)),
        name="norm_matmul_qk" if qk_norm else "norm_matmul",
    )(*args)


def _matmul_res_kernel(a_ref, w_ref, r_ref, o_ref):
    o_ref[...] = r_ref[...] + jnp.dot(a_ref[...], w_ref[...], preferred_element_type=F32)


def _matmul_res(a, w, layer, res, *, tm=512):
    m, k = a.shape
    n = w.shape[2]
    tm = min(tm, m)
    stream = _nbytes((tm, k), BF16) + 2 * _nbytes((tm, n), F32)
    return pl.pallas_call(
        _matmul_res_kernel,
        grid=(m // tm,),
        in_specs=[pl.BlockSpec((tm, k), lambda i: (i, 0)),
                  pl.BlockSpec((None, k, n), lambda i: (layer, 0, 0), pipeline_mode=pl.Buffered(1)),
                  pl.BlockSpec((tm, n), lambda i: (i, 0))],
        out_specs=pl.BlockSpec((tm, n), lambda i: (i, 0)),
        out_shape=jax.ShapeDtypeStruct((m, n), F32),
        compiler_params=_params(1, stream, _nbytes((k, n), BF16), _nbytes((tm, n), F32)),
        name="matmul_res",
    )(a, w, res)


def _mlp_kernel(x_ref, g_ref, w1_ref, w2_ref, o_ref, xn_ref):
    j = pl.program_id(1)

    @pl.when(j == 0)
    def _():
        x = x_ref[...]
        xn_ref[...] = (x * _rms_scale(x) * g_ref[...]).astype(BF16)
        o_ref[...] = x

    h = jnp.dot(xn_ref[...], w1_ref[...], preferred_element_type=F32)
    h = jnp.maximum(h, 0.0)
    o_ref[...] += jnp.dot((h * h).astype(BF16), w2_ref[...], preferred_element_type=F32)


def _mlp(x, g, w1, w2, layer, *, tm=1024, th=512):
    m, d = x.shape
    hid = w1.shape[2]
    tm = min(tm, m)
    blk = 2 * _nbytes((tm, d), F32) + 2 * _nbytes((d, th), BF16)
    return pl.pallas_call(
        _mlp_kernel,
        grid=(m // tm, hid // th),
        in_specs=[pl.BlockSpec((tm, d), lambda i, j: (i, 0)),
                  pl.BlockSpec((1, d), lambda i, j: (0, 0)),
                  pl.BlockSpec((None, d, th), lambda i, j: (layer, 0, j)),
                  pl.BlockSpec((None, th, d), lambda i, j: (layer, j, 0))],
        out_specs=pl.BlockSpec((tm, d), lambda i, j: (i, 0)),
        out_shape=jax.ShapeDtypeStruct((m, d), F32),
        scratch_shapes=[pltpu.VMEM((tm, d), BF16)],
        compiler_params=_params(2, blk, _nbytes((tm, d), BF16),
                                2 * _nbytes((tm, th), F32) + _nbytes((tm, d), F32)),
        name="sq_relu_mlp",
    )(x, g.reshape(1, d), w1, w2)


def _na_window_plan(tile_kind, rr):
    half = WIN_H // 2
    clamped = (tile_kind == 0 and rr < half) or (tile_kind == 2 and rr > half)
    if clamped:
        return half // 2, rr
    if rr % 2 == 0:
        return rr // 2, half
    return rr // 2, WIN_H


def _na_kernel(q_ref, k0, k1, k2, k3, v0, v1, v2, v3, bias_ref, o_ref, kbuf, vbuf, pbuf, *, n_row_tiles):
    t = pl.program_id(2)
    n_heads = q_ref.shape[1] // LANES
    quarter = kbuf.shape[0] // 4
    n_keys = kbuf.shape[0]
    win = bias_ref.shape[3]
    ones = jnp.ones((quarter, LANES), BF16)
    for n, (kr, vr) in enumerate(((k0, v0), (k1, v1), (k2, v2), (k3, v3))):
        rows_n = slice(n * quarter, (n + 1) * quarter)
        kbuf[rows_n, :] = kr[...]
        for hh in range(n_heads):
            vbuf[rows_n, 2 * hh * LANES:(2 * hh + 1) * LANES] = vr[:, hh * LANES:(hh + 1) * LANES]
            vbuf[rows_n, (2 * hh + 1) * LANES:(2 * hh + 2) * LANES] = ones

    def tile(tile_kind):
        for hh in range(n_heads):
            cols = slice(hh * LANES, (hh + 1) * LANES)
            s_all = lax.dot_general(q_ref[:, cols], kbuf[:, cols], (((1,), (1,)), ((), ())),
                                    preferred_element_type=F32)
            for rr in range(NA_ROWS_PER_TILE):
                blk0, slab = _na_window_plan(tile_kind, rr)
                lo = blk0 * LANES
                qrows = slice(rr * GRID_W, (rr + 1) * GRID_W)
                s = s_all[qrows, lo:lo + win] + bias_ref[hh, slab]
                p = jnp.exp(s - jnp.max(s, axis=-1, keepdims=True)).astype(BF16)
                if lo > 0:
                    pbuf[hh, qrows, 0:lo] = jnp.zeros((GRID_W, lo), BF16)
                pbuf[hh, qrows, lo:lo + win] = p
                if lo + win < n_keys:
                    pbuf[hh, qrows, lo + win:n_keys] = jnp.zeros((GRID_W, n_keys - lo - win), BF16)
            o = jnp.dot(pbuf[hh], vbuf[:, 2 * hh * LANES:(2 * hh + 2) * LANES], preferred_element_type=F32)
            o_ref[:, cols] = (o[:, :LANES] / o[:, LANES:]).astype(o_ref.dtype)

    @pl.when(t == 0)
    def _():
        tile(0)

    @pl.when(jnp.logical_and(t > 0, t < n_row_tiles - 1))
    def _():
        tile(1)

    @pl.when(t == n_row_tiles - 1)
    def _():
        tile(2)


def _na_attention(qkv, bias_tab, batch, seq):
    m = qkv.shape[0]
    d = qkv.shape[1] // 3
    rows = seq // GRID_W
    hg = NA_HEADS_PER_TILE
    cw = hg * LANES
    ncb = d // cw
    tq = NA_ROWS_PER_TILE * GRID_W
    tk = tq // 2
    rt = rows // NA_ROWS_PER_TILE
    assert rt >= 2
    nkb = seq // tk
    n_keys = NA_KEY_ROWS * GRID_W
    n_slabs, win = bias_tab.shape[1], bias_tab.shape[3]

    def kv_spec(n, region):
        def imap(b, g, t):
            return (b * nkb + jnp.clip(2 * t - 1 + n, 0, nkb - 1), region * ncb + g)
        return pl.BlockSpec((tk, cw), imap)

    in_specs = [pl.BlockSpec((tq, cw), lambda b, g, t: (b * rt + t, g))]
    in_specs += [kv_spec(n, 1) for n in range(4)] + [kv_spec(n, 2) for n in range(4)]
    in_specs.append(pl.BlockSpec((hg, n_slabs, GRID_W, win), lambda b, g, t: (g, 0, 0, 0)))
    blk = 2 * _nbytes((tq, cw), BF16) + 8 * _nbytes((tk, cw), BF16) + _nbytes((hg, n_slabs, GRID_W, win), F32)
    scratch_shapes = [pltpu.VMEM((n_keys, cw), BF16), pltpu.VMEM((n_keys, 2 * cw), BF16),
                      pltpu.VMEM((hg, tq, n_keys), BF16)]
    scratch = _nbytes((n_keys, 3 * cw), BF16) + _nbytes((hg, tq, n_keys), BF16)
    return pl.pallas_call(
        functools.partial(_na_kernel, n_row_tiles=rt),
        grid=(batch, ncb, rt),
        in_specs=in_specs,
        out_specs=pl.BlockSpec((tq, cw), lambda b, g, t: (b * rt + t, g)),
        out_shape=jax.ShapeDtypeStruct((m, d), BF16),
        scratch_shapes=scratch_shapes,
        compiler_params=_params(3, blk, scratch, 3 * _nbytes((tq, n_keys), F32)),
        name="na_attention",
    )(qkv, *([qkv] * 8), bias_tab)


def _na_bias_table(rel_bias):
    d = np.arange(WIN_H)[:, None]
    i = np.arange(WIN_H)[None, :]
    row_sel = (i - d + (WIN_H - 1))[:, :, None] == np.arange(2 * WIN_H - 1)
    c = np.arange(GRID_W)[:, None]
    kc = np.arange(GRID_W)[None, :]
    c0 = np.clip(c - WIN_W // 2, 0, GRID_W - WIN_W)
    valid = (kc >= c0) & (kc < c0 + WIN_W)
    col_sel = ((kc - c + (WIN_W - 1))[:, :, None] == np.arange(2 * WIN_W - 1)) & valid[:, :, None]
    tab = jnp.einsum("hrs,dir,cks->hdcik", rel_bias.astype(F32), row_sel.astype(np.float32),
                     col_sel.astype(np.float32), precision=lax.Precision.HIGHEST)
    tab = jnp.where(valid[None, None, :, None, :], tab, -jnp.inf)
    tab = tab.reshape(rel_bias.shape[0], WIN_H, GRID_W, WIN_H * GRID_W)

    def pad(a, lo, hi):
        return jnp.pad(a, ((0, 0), (0, 0), (0, 0), (lo, hi)), constant_values=-jnp.inf)

    shifted = pad(tab[:, WIN_H // 2:WIN_H // 2 + 1], LANES // 2, LANES // 2)
    return jnp.concatenate([pad(tab, 0, LANES), shifted], axis=1)


def _na_layer(x, g, w_qkv, j, q_gain, k_gain, rel_bias, w_o, batch, seq):
    head_dim = w_qkv.shape[1] // NA_HEADS
    gains = jnp.stack([q_gain.astype(F32) * (head_dim ** -0.5), k_gain.astype(F32),
                       jnp.ones_like(q_gain, F32)]).reshape(3, 1, head_dim)
    qkv = _norm_matmul(x, g, w_qkv, j, gains)
    att = _na_attention(qkv, _na_bias_table(rel_bias), batch, seq)
    return _matmul_res(att, w_o, j, x)


def _ml_pre_kernel(xm_ref, prev_ref, next_ref, cw_ref, cb_ref, wq_ref, wk_ref, wv_ref, wg_ref, gb_ref,
                   q_ref, k_ref, v_ref, xc_ref, g_ref, *, seq, k_scale):
    i = pl.program_id(0)
    j = pl.program_id(1)
    tm = xm_ref.shape[0]
    xm = xm_ref[...]
    x = xm.astype(F32)
    t0 = (i * tm) % seq
    prev = prev_ref[...].astype(F32)
    nxt = next_ref[...].astype(F32)
    prev_row = jnp.where(t0 == 0, 0.0, prev[prev.shape[0] - 1:, :])
    nxt = jnp.where(t0 + tm == seq, 0.0, nxt[0:2, :])
    row = lax.broadcasted_iota(jnp.int32, (tm, 1), 0)
    x_m1 = jnp.where(row == 0, prev_row, pltpu.roll(x, 1, axis=0))
    x_p1 = jnp.where(row == tm - 1, nxt[0:1], pltpu.roll(x, tm - 1, axis=0))
    x_p2 = jnp.where(row == tm - 2, nxt[0:1],
                     jnp.where(row == tm - 1, nxt[1:2], pltpu.roll(x, tm - 2, axis=0)))
    cw = cw_ref[...]
    xc = x_m1 * cw[0:1] + x * cw[1:2] + x_p1 * cw[2:3] + x_p2 * cw[3:4] + cb_ref[...]
    xc = _silu(xc)
    xcb = xc.astype(BF16)
    xc_ref[...] = xcb

    part = jnp.zeros(g_ref.shape, F32)
    for c in range(xm.shape[1] // MXU_DIM):
        sl = slice(c * MXU_DIM, (c + 1) * MXU_DIM)
        qc = jnp.dot(xcb[:, sl], wq_ref[c], preferred_element_type=F32).astype(BF16)
        kc = jnp.dot(xcb[:, sl], wk_ref[c], preferred_element_type=F32).astype(BF16)
        vc = jnp.dot(xm[:, sl], wv_ref[c], preferred_element_type=F32).astype(BF16)
        q_ref[:, sl] = qc
        k_ref[:, sl] = kc * k_scale
        v_ref[:, sl] = vc
        part += jnp.dot(qc, wg_ref[0, sl, :], preferred_element_type=F32)
        part += jnp.dot(kc, wg_ref[1, sl, :], preferred_element_type=F32)
        part += jnp.dot(vc, wg_ref[2, sl, :], preferred_element_type=F32)

    @pl.when(j == 0)
    def _():
        g_ref[...] = gb_ref[...] + part

    @pl.when(j > 0)
    def _():
        g_ref[...] += part


def _ml_pre(up, conv_w, conv_b, wq_bd, wk_bd, wv_bd, w_gates, b_gates, seq, *, tm=512, halo=16):
    m = up.shape[0]
    inner = up.shape[1] // 2
    tm = min(tm, seq)
    tf = inner // ML_HEADS
    nfb = inner // tf
    gpb = tf // MXU_DIM
    hb = tm // halo
    nhalo = m // halo
    head_dim = tf
    k_scale = head_dim ** -0.5
    assert 2.0 ** round(np.log2(k_scale)) == k_scale
    in_specs = [
        pl.BlockSpec((tm, tf), lambda i, j: (i, j)),
        pl.BlockSpec((halo, tf), lambda i, j: (jnp.maximum(i * hb - 1, 0), j)),
        pl.BlockSpec((halo, tf), lambda i, j: (jnp.minimum((i + 1) * hb, nhalo - 1), j)),
        pl.BlockSpec((ML_CONV_K, tf), lambda i, j: (0, j)),
        pl.BlockSpec((1, tf), lambda i, j: (0, j)),
        pl.BlockSpec((gpb, MXU_DIM, MXU_DIM), lambda i, j: (j, 0, 0)),
        pl.BlockSpec((gpb, MXU_DIM, MXU_DIM), lambda i, j: (j, 0, 0)),
        pl.BlockSpec((gpb, MXU_DIM, MXU_DIM), lambda i, j: (j, 0, 0)),
        pl.BlockSpec((3, tf, LANES), lambda i, j: (0, j, 0)),
        pl.BlockSpec((1, LANES), lambda i, j: (0, 0)),
    ]
    act = jax.ShapeDtypeStruct((m, inner), BF16)
    act_spec = pl.BlockSpec((tm, tf), lambda i, j: (i, j))
    blk = (5 * _nbytes((tm, tf), BF16) + 3 * _nbytes((gpb, MXU_DIM, MXU_DIM), BF16)
           + _nbytes((3, tf, LANES), BF16) + _nbytes((tm, LANES), F32))
    return pl.pallas_call(
        functools.partial(_ml_pre_kernel, seq=seq, k_scale=k_scale),
        grid=(m // tm, nfb),
        in_specs=in_specs,
        out_specs=[act_spec, act_spec, act_spec, act_spec, pl.BlockSpec((tm, LANES), lambda i, j: (i, 0))],
        out_shape=[act, act, act, act, jax.ShapeDtypeStruct((m, LANES), F32)],
        compiler_params=_params(2, blk, 0, 8 * _nbytes((tm, tf), F32)),
        name="mlstm_pre",
    )(up, up, up, conv_w, conv_b.reshape(1, inner), wq_bd, wk_bd, wv_bd, w_gates, b_gates)


def _split_dot(a, tri):
    hi = a.astype(BF16)
    r1 = a - hi.astype(F32)
    mid = r1.astype(BF16)
    lo = (r1 - mid.astype(F32)).astype(BF16)
    return (jnp.dot(hi, tri, preferred_element_type=F32) + jnp.dot(mid, tri, preferred_element_type=F32)
            + jnp.dot(lo, tri, preferred_element_type=F32))


def _gate_cumsum_kernel(f_ref, b_ref, *, n_chunks, n_dir_heads):
    f = f_ref[...]
    lf = jnp.minimum(f, 0.0) - jnp.log(1.0 + jnp.exp(-jnp.abs(f)))
    lc = f.shape[1]
    s = lax.broadcasted_iota(jnp.int32, (lc, lc), 0)
    l = lax.broadcasted_iota(jnp.int32, (lc, lc), 1)
    prefix = _split_dot(lf, (s <= l).astype(BF16))
    suffix = _split_dot(lf, (s >= l).astype(BF16))
    row = lax.broadcasted_iota(jnp.int32, (f.shape[0], 1), 0)
    backward = ((row // n_chunks) % (2 * n_dir_heads)) >= n_dir_heads
    b_ref[...] = jnp.where(backward, suffix, prefix)


def _gate_cumsum(f_rows, n_chunks):
    r, lc = f_rows.shape
    return pl.pallas_call(
        functools.partial(_gate_cumsum_kernel, n_chunks=n_chunks, n_dir_heads=ML_HEADS),
        grid=(1,),
        in_specs=[pl.BlockSpec((r, lc), lambda i: (0, 0))],
        out_specs=pl.BlockSpec((r, lc), lambda i: (0, 0)),
        out_shape=jax.ShapeDtypeStruct((r, lc), F32),
        compiler_params=_params(1, 2 * _nbytes((r, lc), F32), 0, 8 * _nbytes((r, lc), F32)),
        name="mlstm_gate_cumsum",
    )(f_rows)


def _ml_core_kernel(q_ref, k_ref, v_ref, ir_ref, br_ref, ic_ref, bc_ref, *rest, reverse, combine):
    if combine:
        hf_ref, xc_ref, z_ref, on_ref, sk_ref, h_ref, c_ref, n_ref, m_ref = rest
    else:
        h_ref, c_ref, n_ref, m_ref = rest
    hh = pl.program_id(1)
    step = pl.program_id(2)
    lc = q_ref.shape[0]

    @pl.when(step == 0)
    def _():
        c_ref[...] = jnp.zeros_like(c_ref)
        n_ref[...] = jnp.zeros_like(n_ref)
        m_ref[...] = jnp.zeros_like(m_ref)

    gate_col = hh + (ML_HEADS if reverse else 0)
    pick = lax.broadcasted_iota(jnp.int32, ic_ref.shape, 1) == gate_col
    i_col = jnp.sum(jnp.where(pick, ic_ref[...], 0.0), axis=1, keepdims=True)
    b_col = jnp.sum(jnp.where(pick, bc_ref[...], 0.0), axis=1, keepdims=True)
    i_row = ir_ref[0]
    b_row = br_ref[0]
    g = b_row[:, 0:1] if reverse else b_row[:, lc - 1:lc]
    m = m_ref[...]

    l_idx = lax.broadcasted_iota(jnp.int32, (lc, lc), 0)
    s_idx = lax.broadcasted_iota(jnp.int32, (lc, lc), 1)
    visible = (s_idx >= l_idx) if reverse else (s_idx <= l_idx)
    log_d = jnp.where(visible, b_col - b_row + i_row, -jnp.inf)
    m_inter = b_col + m
    m_row = jnp.maximum(m_inter, jnp.max(log_d, axis=1, keepdims=True))
    q = q_ref[...]
    k = k_ref[...]
    v = v_ref[...]
    s = lax.dot_general(q, k, (((1,), (1,)), ((), ())), preferred_element_type=F32) * jnp.exp(log_d - m_row)
    w_inter = jnp.exp(m_inter - m_row)
    q_c = jnp.dot(q, c_ref[...].astype(BF16), preferred_element_type=F32)
    q_n = jnp.sum(q.astype(F32) * n_ref[...], axis=1, keepdims=True)
    num = jnp.dot(s.astype(BF16), v, preferred_element_type=F32) + w_inter * q_c
    den = jnp.sum(s, axis=1, keepdims=True) + w_inter * q_n
    h = num / jnp.maximum(jnp.abs(den), jnp.exp(-m_row))
    if combine:
        hc = h + hf_ref[...].astype(F32)
        cen = hc - jnp.mean(hc, axis=-1, keepdims=True)
        var = jnp.mean(cen * cen, axis=-1, keepdims=True)
        hn = cen * lax.rsqrt(var + LN_EPS) * on_ref[...]
        h = (hn + sk_ref[...] * xc_ref[...].astype(F32)) * _silu(z_ref[...].astype(F32))
    h_ref[...] = h.astype(h_ref.dtype)

    log_w = g - b_col + i_col
    m_new = jnp.maximum(g + m, jnp.max(log_w, axis=0, keepdims=True))
    wk = k.astype(F32) * jnp.exp(log_w - m_new)
    decay = jnp.exp(g + m - m_new)
    upd = lax.dot_general(wk.astype(BF16), v, (((0,), (0,)), ((), ())), preferred_element_type=F32)
    c_ref[...] = decay * c_ref[...] + upd
    n_ref[...] = decay * n_ref[...] + jnp.sum(wk, axis=0, keepdims=True)
    m_ref[...] = m_new


def _ml_core(q, k, v, i_rows, b_rows, i_cols, b_cols, batch, seq, *, reverse, combine=None):
    m, inner = q.shape
    lc = ML_CHUNK
    nc = seq // lc
    dh = inner // ML_HEADS
    n_gate_cols = 2 * ML_HEADS

    def chunk(c):
        return nc - 1 - c if reverse else c

    def act_map(b, h, c):
        return (b * nc + chunk(c), h)

    def row_map(b, h, c):
        return ((b * n_gate_cols + h + (ML_HEADS if reverse else 0)) * nc + chunk(c), 0, 0)

    def col_map(b, h, c):
        return (b * nc + chunk(c), 0)

    act_spec = pl.BlockSpec((lc, dh), act_map)
    row_spec = pl.BlockSpec((1, 1, lc), row_map)
    col_spec = pl.BlockSpec((lc, LANES), col_map)
    blk = 4 * _nbytes((lc, dh), BF16) + 2 * _nbytes((8, lc), F32) + 2 * _nbytes((lc, LANES), F32)
    scratch = _nbytes((dh, dh), F32) + 2 * _nbytes((8, dh), F32)
    in_specs = [act_spec, act_spec, act_spec, row_spec, row_spec, col_spec, col_spec]
    args = [q, k, v, i_rows, b_rows, i_cols, b_cols]
    temp = 3 * _nbytes((dh, dh), F32)
    if combine is not None:
        h_other, x_c, up, out_norm, skip = combine
        z_spec = pl.BlockSpec((lc, dh), lambda b, h, c: (b * nc + chunk(c), ML_HEADS + h))
        vec_spec = pl.BlockSpec((1, dh), lambda b, h, c: (0, h))
        in_specs += [act_spec, act_spec, z_spec, vec_spec, vec_spec]
        args += [h_other, x_c, up, out_norm.reshape(1, inner), skip.reshape(1, inner)]
        blk += 3 * _nbytes((lc, dh), BF16)
        temp += 4 * _nbytes((lc, dh), F32)
    return pl.pallas_call(
        functools.partial(_ml_core_kernel, reverse=reverse, combine=combine is not None),
        grid=(batch, ML_HEADS, nc),
        in_specs=in_specs,
        out_specs=act_spec,
        out_shape=jax.ShapeDtypeStruct((m, inner), BF16),
        scratch_shapes=[pltpu.VMEM((dh, dh), F32), pltpu.VMEM((1, dh), F32), pltpu.VMEM((1, 1), F32)],
        compiler_params=_params(3, blk, scratch, temp),
        name="mlstm_core_bwd" if reverse else "mlstm_core_fwd",
    )(*args)


def _block_diag_tiles(w):
    per = MXU_DIM // ML_QKV_BLOCK
    w4 = w.reshape(-1, per, ML_QKV_BLOCK, ML_QKV_BLOCK)
    eye = jnp.eye(per, dtype=w.dtype)
    return jnp.einsum("ngio,gh->ngiho", w4, eye).reshape(-1, MXU_DIM, MXU_DIM).astype(BF16)


def _ml_layer(x, g, w_up, j, conv_w, conv_b, w_q, w_k, w_v, w_ig, b_ig, w_fg, b_fg, out_norm, skip, w_down,
              batch, seq):
    m = x.shape[0]
    inner = w_up.shape[2] // 2
    n_gate_cols = 2 * ML_HEADS
    nc = seq // ML_CHUNK
    up = _norm_matmul(x, g, w_up, j)
    w_gates = jnp.concatenate([w_ig, w_fg], axis=1)
    w_gates = jnp.pad(w_gates, ((0, 0), (0, LANES - 2 * n_gate_cols))).reshape(3, inner, LANES).astype(BF16)
    b_gates = jnp.pad(jnp.concatenate([b_ig, b_fg]), (0, LANES - 2 * n_gate_cols)).reshape(1, LANES).astype(F32)
    q, k, v, x_c, gates = _ml_pre(up, conv_w.astype(F32), conv_b.astype(F32), _block_diag_tiles(w_q),
                                  _block_diag_tiles(w_k), _block_diag_tiles(w_v), w_gates, b_gates, seq)

    def to_rows(cols):
        return cols.reshape(batch, nc, ML_CHUNK, n_gate_cols).transpose(0, 3, 1, 2).reshape(-1, ML_CHUNK)

    def to_cols(rows):
        cols = rows.reshape(batch, n_gate_cols, nc, ML_CHUNK).transpose(0, 2, 3, 1).reshape(m, n_gate_cols)
        return jnp.pad(cols, ((0, 0), (0, LANES - n_gate_cols)))

    i_rows = to_rows(gates[:, :n_gate_cols])
    b_rows = _gate_cumsum(to_rows(gates[:, n_gate_cols:2 * n_gate_cols]), nc)
    b_cols = to_cols(b_rows)
    i_rows3 = i_rows.reshape(-1, 1, ML_CHUNK)
    b_rows3 = b_rows.reshape(-1, 1, ML_CHUNK)
    h_f = _ml_core(q, k, v, i_rows3, b_rows3, gates, b_cols, batch, seq, reverse=False)
    y = _ml_core(q, k, v, i_rows3, b_rows3, gates, b_cols, batch, seq, reverse=True,
                 combine=(h_f, x_c, up, out_norm.astype(F32), skip.astype(F32)))
    return _matmul_res(y, w_down, j, x)


def kernel(x, norm_mix, norm_mlp, na_w_qkv, na_q_gain, na_k_gain, na_rel_bias, na_w_o, ml_w_up, ml_conv_w, ml_conv_b, ml_w_q, ml_w_k, ml_w_v, ml_w_ig, ml_b_ig, ml_w_fg, ml_b_fg, ml_out_norm, ml_skip, ml_w_down, mlp_w1, mlp_w2):
    batch, seq, d = x.shape
    depth = norm_mix.shape[0]
    n_mixers = 2
    na_w_qkv, na_w_o, ml_w_up, ml_w_down, mlp_w1, mlp_w2 = (
        w.astype(BF16) for w in (na_w_qkv, na_w_o, ml_w_up, ml_w_down, mlp_w1, mlp_w2))
    h = x.reshape(batch * seq, d).astype(F32)
    for layer in range(depth):
        j = layer // n_mixers
        if layer % n_mixers == 0:
            h = _na_layer(h, norm_mix[layer], na_w_qkv, j, na_q_gain[j], na_k_gain[j], na_rel_bias[j],
                          na_w_o, batch, seq)
        else:
            h = _ml_layer(h, norm_mix[layer], ml_w_up, j, ml_conv_w[j], ml_conv_b[j], ml_w_q[j], ml_w_k[j],
                          ml_w_v[j], ml_w_ig[j], ml_b_ig[j], ml_w_fg[j], ml_b_fg[j], ml_out_norm[j],
                          ml_skip[j], ml_w_down, batch, seq)
        h = _mlp(h, norm_mlp[layer], mlp_w1, mlp_w2, layer)
    return h.reshape(batch, seq, d).astype(x.dtype)
```

```python
import functools

import numpy as np
import jax
import jax.numpy as jnp
from jax import lax
from jax.experimental import pallas as pl
from jax.experimental.pallas import tpu as pltpu

GRID_W = 64
NA_HEADS = 16
WIN_H = 8
WIN_W = 16
ML_HEADS = 4
ML_QKV_BLOCK = 4
ML_CONV_K = 4
RMS_EPS = 1e-6
LN_EPS = 1e-5

LANES = 128
MXU_DIM = 256
VMEM_BYTES_V7X = 64 * 1024 * 1024
VMEM_COMPILER_RESERVE = 6 * 1024 * 1024

ML_CHUNK = 512
NA_ROWS_PER_TILE = 8
NA_HEADS_PER_TILE = 4
NA_KEY_ROWS = 2 * NA_ROWS_PER_TILE

F32 = jnp.float32
BF16 = jnp.bfloat16


def _nbytes(shape, dtype):
    return int(np.prod(shape)) * jnp.dtype(dtype).itemsize


def _params(n_grid_axes, block_bytes, scratch_bytes=0, temp_bytes=0):
    need = 2 * block_bytes + scratch_bytes + temp_bytes + VMEM_COMPILER_RESERVE
    limit = min(max(need, 16 * 1024 * 1024), VMEM_BYTES_V7X - 2 * 1024 * 1024)
    return pltpu.CompilerParams(dimension_semantics=("arbitrary",) * n_grid_axes,
                                vmem_limit_bytes=int(limit))


def _rms_scale(x):
    return lax.rsqrt(jnp.mean(x * x, axis=-1, keepdims=True) + RMS_EPS)


def _silu(x):
    return x * (1.0 / (1.0 + jnp.exp(-x)))


def _norm_matmul_kernel(*refs, qk_norm, tn):
    if qk_norm:
        x_ref, g_ref, w_ref, hg_ref, o_ref = refs
    else:
        x_ref, g_ref, w_ref, o_ref = refs
    x = x_ref[...]
    xn = (x * _rms_scale(x) * g_ref[...]).astype(BF16)
    n = o_ref.shape[1]
    for c in range(n // tn):
        cols = slice(c * tn, (c + 1) * tn)
        y = jnp.dot(xn, w_ref[:, cols], preferred_element_type=F32)
        region = (3 * c * tn) // n
        if qk_norm and region < 2:
            gain = hg_ref[region]
            for cc in range(tn // LANES):
                blk = y[:, cc * LANES:(cc + 1) * LANES]
                o_ref[:, c * tn + cc * LANES:c * tn + (cc + 1) * LANES] = (
                    blk * _rms_scale(blk) * gain).astype(o_ref.dtype)
        else:
            o_ref[:, cols] = y.astype(o_ref.dtype)


def _norm_matmul(x, g, w, layer, head_gains=None, *, tm, tn=1024):
    m, k = x.shape
    n = w.shape[2]
    tm = min(tm, m)
    qk_norm = head_gains is not None
    assert n % tn == 0 and (not qk_norm or (n // 3) % tn == 0)
    in_specs = [pl.BlockSpec((tm, k), lambda i: (i, 0)),
                pl.BlockSpec((1, k), lambda i: (0, 0)),
                pl.BlockSpec((None, k, n), lambda i: (layer, 0, 0), pipeline_mode=pl.Buffered(1))]
    args = [x, g.reshape(1, k), w]
    if qk_norm:
        in_specs.append(pl.BlockSpec(head_gains.shape, lambda i: (0, 0, 0)))
        args.append(head_gains)
    stream = _nbytes((tm, k), F32) + _nbytes((tm, n), BF16)
    return pl.pallas_call(
        functools.partial(_norm_matmul_kernel, qk_norm=qk_norm, tn=tn),
        grid=(m // tm,),
        in_specs=in_specs,
        out_specs=pl.BlockSpec((tm, n), lambda i: (i, 0)),
        out_shape=jax.ShapeDtypeStruct((m, n), BF16),
        compiler_params=_params(1, stream, _nbytes((k, n), BF16),
                                _nbytes((tm, k), BF16) + 2 * _nbytes((tm, tn), F32)),
        name="norm_matmul_qk" if qk_norm else "norm_matmul",
    )(*args)


def _matmul_res_kernel(a_ref, w_ref, r_ref, o_ref):
    o_ref[...] = r_ref[...] + jnp.dot(a_ref[...], w_ref[...], preferred_element_type=F32)


def _matmul_res(a, w, layer, res, *, tm=512):
    m, k = a.shape
    n = w.shape[2]
    tm = min(tm, m)
    stream = _nbytes((tm, k), BF16) + 2 * _nbytes((tm, n), F32)
    return pl.pallas_call(
        _matmul_res_kernel,
        grid=(m // tm,),
        in_specs=[pl.BlockSpec((tm, k), lambda i: (i, 0)),
                  pl.BlockSpec((None, k, n), lambda i: (layer, 0, 0), pipeline_mode=pl.Buffered(1)),
                  pl.BlockSpec((tm, n), lambda i: (i, 0))],
        out_specs=pl.BlockSpec((tm, n), lambda i: (i, 0)),
        out_shape=jax.ShapeDtypeStruct((m, n), F32),
        compiler_params=_params(1, stream, _nbytes((k, n), BF16), _nbytes((tm, n), F32)),
        name="matmul_res",
    )(a, w, res)


def _mlp_kernel(x_ref, g_ref, w1_ref, w2_ref, o_ref, xn_ref):
    j = pl.program_id(1)

    @pl.when(j == 0)
    def _():
        x = x_ref[...]
        xn_ref[...] = (x * _rms_scale(x) * g_ref[...]).astype(BF16)
        o_ref[...] = x

    h = jnp.dot(xn_ref[...], w1_ref[...], preferred_element_type=F32)
    h = jnp.maximum(h, 0.0)
    o_ref[...] += jnp.dot((h * h).astype(BF16), w2_ref[...], preferred_element_type=F32)


def _mlp(x, g, w1, w2, layer, *, tm=1024, th=512):
    m, d = x.shape
    hid = w1.shape[2]
    tm = min(tm, m)
    blk = 2 * _nbytes((tm, d), F32) + 2 * _nbytes((d, th), BF16)
    return pl.pallas_call(
        _mlp_kernel,
        grid=(m // tm, hid // th),
        in_specs=[pl.BlockSpec((tm, d), lambda i, j: (i, 0)),
                  pl.BlockSpec((1, d), lambda i, j: (0, 0)),
                  pl.BlockSpec((None, d, th), lambda i, j: (layer, 0, j)),
                  pl.BlockSpec((None, th, d), lambda i, j: (layer, j, 0))],
        out_specs=pl.BlockSpec((tm, d), lambda i, j: (i, 0)),
        out_shape=jax.ShapeDtypeStruct((m, d), F32),
        scratch_shapes=[pltpu.VMEM((tm, d), BF16)],
        compiler_params=_params(2, blk, _nbytes((tm, d), BF16),
                                2 * _nbytes((tm, th), F32) + _nbytes((tm, d), F32)),
        name="sq_relu_mlp",
    )(x, g.reshape(1, d), w1, w2)


def _na_window_plan(tile_kind, rr):
    half = WIN_H // 2
    clamped = (tile_kind == 0 and rr < half) or (tile_kind == 2 and rr > half)
    if clamped:
        return half // 2, rr
    if rr % 2 == 0:
        return rr // 2, half
    return rr // 2, WIN_H


def _na_kernel(q_ref, k0, k1, k2, k3, v0, v1, v2, v3, bias_ref, o_ref, kbuf, vbuf, pbuf, *, n_row_tiles):
    t = pl.program_id(2)
    n_heads = q_ref.shape[1] // LANES
    quarter = kbuf.shape[0] // 4
    n_keys = kbuf.shape[0]
    win = bias_ref.shape[3]
    ones = jnp.ones((quarter, LANES), BF16)
    for n, (kr, vr) in enumerate(((k0, v0), (k1, v1), (k2, v2), (k3, v3))):
        rows_n = slice(n * quarter, (n + 1) * quarter)
        kbuf[rows_n, :] = kr[...]
        for hh in range(n_heads):
            vbuf[rows_n, 2 * hh * LANES:(2 * hh + 1) * LANES] = vr[:, hh * LANES:(hh + 1) * LANES]
            vbuf[rows_n, (2 * hh + 1) * LANES:(2 * hh + 2) * LANES] = ones

    def tile(tile_kind):
        for hh in range(n_heads):
            cols = slice(hh * LANES, (hh + 1) * LANES)
            s_all = lax.dot_general(q_ref[:, cols], kbuf[:, cols], (((1,), (1,)), ((), ())),
                                    preferred_element_type=F32)
            for rr in range(NA_ROWS_PER_TILE):
                blk0, slab = _na_window_plan(tile_kind, rr)
                lo = blk0 * LANES
                qrows = slice(rr * GRID_W, (rr + 1) * GRID_W)
                s = s_all[qrows, lo:lo + win] + bias_ref[hh, slab]
                p = jnp.exp(s - jnp.max(s, axis=-1, keepdims=True)).astype(BF16)
                if lo > 0:
                    pbuf[hh, qrows, 0:lo] = jnp.zeros((GRID_W, lo), BF16)
                pbuf[hh, qrows, lo:lo + win] = p
                if lo + win < n_keys:
                    pbuf[hh, qrows, lo + win:n_keys] = jnp.zeros((GRID_W, n_keys - lo - win), BF16)
            o = jnp.dot(pbuf[hh], vbuf[:, 2 * hh * LANES:(2 * hh + 2) * LANES], preferred_element_type=F32)
            o_ref[:, cols] = (o[:, :LANES] / o[:, LANES:]).astype(o_ref.dtype)

    @pl.when(t == 0)
    def _():
        tile(0)

    @pl.when(jnp.logical_and(t > 0, t < n_row_tiles - 1))
    def _():
        tile(1)

    @pl.when(t == n_row_tiles - 1)
    def _():
        tile(2)


def _na_attention(qkv, bias_tab, batch, seq):
    m = qkv.shape[0]
    d = qkv.shape[1] // 3
    rows = seq // GRID_W
    hg = NA_HEADS_PER_TILE
    cw = hg * LANES
    ncb = d // cw
    tq = NA_ROWS_PER_TILE * GRID_W
    tk = tq // 2
    rt = rows // NA_ROWS_PER_TILE
    assert rt >= 2
    nkb = seq // tk
    n_keys = NA_KEY_ROWS * GRID_W
    n_slabs, win = bias_tab.shape[1], bias_tab.shape[3]

    def kv_spec(n, region):
        def imap(b, g, t):
            return (b * nkb + jnp.clip(2 * t - 1 + n, 0, nkb - 1), region * ncb + g)
        return pl.BlockSpec((tk, cw), imap)

    in_specs = [pl.BlockSpec((tq, cw), lambda b, g, t: (b * rt + t, g))]
    in_specs += [kv_spec(n, 1) for n in range(4)] + [kv_spec(n, 2) for n in range(4)]
    in_specs.append(pl.BlockSpec((hg, n_slabs, GRID_W, win), lambda b, g, t: (g, 0, 0, 0)))
    blk = 2 * _nbytes((tq, cw), BF16) + 8 * _nbytes((tk, cw), BF16) + _nbytes((hg, n_slabs, GRID_W, win), F32)
    scratch_shapes = [pltpu.VMEM((n_keys, cw), BF16), pltpu.VMEM((n_keys, 2 * cw), BF16),
                      pltpu.VMEM((hg, tq, n_keys), BF16)]
    scratch = _nbytes((n_keys, 3 * cw), BF16) + _nbytes((hg, tq, n_keys), BF16)
    return pl.pallas_call(
        functools.partial(_na_kernel, n_row_tiles=rt),
        grid=(batch, ncb, rt),
        in_specs=in_specs,
        out_specs=pl.BlockSpec((tq, cw), lambda b, g, t: (b * rt + t, g)),
        out_shape=jax.ShapeDtypeStruct((m, d), BF16),
        scratch_shapes=scratch_shapes,
        compiler_params=_params(3, blk, scratch, 3 * _nbytes((tq, n_keys), F32)),
        name="na_attention",
    )(qkv, *([qkv] * 8), bias_tab)


def _na_bias_table(rel_bias):
    d = np.arange(WIN_H)[:, None]
    i = np.arange(WIN_H)[None, :]
    row_sel = (i - d + (WIN_H - 1))[:, :, None] == np.arange(2 * WIN_H - 1)
    c = np.arange(GRID_W)[:, None]
    kc = np.arange(GRID_W)[None, :]
    c0 = np.clip(c - WIN_W // 2, 0, GRID_W - WIN_W)
    valid = (kc >= c0) & (kc < c0 + WIN_W)
    col_sel = ((kc - c + (WIN_W - 1))[:, :, None] == np.arange(2 * WIN_W - 1)) & valid[:, :, None]
    tab = jnp.einsum("hrs,dir,cks->hdcik", rel_bias.astype(F32), row_sel.astype(np.float32),
                     col_sel.astype(np.float32), precision=lax.Precision.HIGHEST)
    tab = jnp.where(valid[None, None, :, None, :], tab, -jnp.inf)
    tab = tab.reshape(rel_bias.shape[0], WIN_H, GRID_W, WIN_H * GRID_W)

    def pad(a, lo, hi):
        return jnp.pad(a, ((0, 0), (0, 0), (0, 0), (lo, hi)), constant_values=-jnp.inf)

    shifted = pad(tab[:, WIN_H // 2:WIN_H // 2 + 1], LANES // 2, LANES // 2)
    return jnp.concatenate([pad(tab, 0, LANES), shifted], axis=1)


def _na_layer(x, g, w_qkv, j, q_gain, k_gain, rel_bias, w_o, batch, seq):
    head_dim = w_qkv.shape[1] // NA_HEADS
    gains = jnp.stack([q_gain.astype(F32) * (head_dim ** -0.5), k_gain.astype(F32)]).reshape(2, 1, head_dim)
    qkv = _norm_matmul(x, g, w_qkv, j, gains, tm=512)
    att = _na_attention(qkv, _na_bias_table(rel_bias), batch, seq)
    return _matmul_res(att, w_o, j, x)


def _ml_pre_kernel(xm_ref, prev_ref, next_ref, cw_ref, cb_ref, wq_ref, wk_ref, wv_ref, wg_ref, gb_ref,
                   q_ref, k_ref, v_ref, xc_ref, g_ref, *, seq, k_scale):
    i = pl.program_id(0)
    j = pl.program_id(1)
    tm = xm_ref.shape[0]
    xm = xm_ref[...]
    x = xm.astype(F32)
    t0 = (i * tm) % seq
    prev = prev_ref[...].astype(F32)
    nxt = next_ref[...].astype(F32)
    prev_row = jnp.where(t0 == 0, 0.0, prev[prev.shape[0] - 1:, :])
    nxt = jnp.where(t0 + tm == seq, 0.0, nxt[0:2, :])
    row = lax.broadcasted_iota(jnp.int32, (tm, 1), 0)
    x_m1 = jnp.where(row == 0, prev_row, pltpu.roll(x, 1, axis=0))
    x_p1 = jnp.where(row == tm - 1, nxt[0:1], pltpu.roll(x, tm - 1, axis=0))
    x_p2 = jnp.where(row == tm - 2, nxt[0:1],
                     jnp.where(row == tm - 1, nxt[1:2], pltpu.roll(x, tm - 2, axis=0)))
    cw = cw_ref[...]
    xc = x_m1 * cw[0:1] + x * cw[1:2] + x_p1 * cw[2:3] + x_p2 * cw[3:4] + cb_ref[...]
    xc = _silu(xc)
    xcb = xc.astype(BF16)
    xc_ref[...] = xcb

    part = jnp.zeros(g_ref.shape, F32)
    for c in range(xm.shape[1] // MXU_DIM):
        sl = slice(c * MXU_DIM, (c + 1) * MXU_DIM)
        qc = jnp.dot(xcb[:, sl], wq_ref[c], preferred_element_type=F32).astype(BF16)
        kc = jnp.dot(xcb[:, sl], wk_ref[c], preferred_element_type=F32).astype(BF16)
        vc = jnp.dot(xm[:, sl], wv_ref[c], preferred_element_type=F32).astype(BF16)
        q_ref[:, sl] = qc
        k_ref[:, sl] = kc * k_scale
        v_ref[:, sl] = vc
        part += jnp.dot(qc, wg_ref[0, sl, :], preferred_element_type=F32)
        part += jnp.dot(kc, wg_ref[1, sl, :], preferred_element_type=F32)
        part += jnp.dot(vc, wg_ref[2, sl, :], preferred_element_type=F32)

    @pl.when(j == 0)
    def _():
        g_ref[...] = gb_ref[...] + part

    @pl.when(j > 0)
    def _():
        g_ref[...] += part


def _ml_pre(up, conv_w, conv_b, wq_bd, wk_bd, wv_bd, w_gates, b_gates, seq, *, tm=512, halo=16):
    m = up.shape[0]
    inner = up.shape[1] // 2
    tm = min(tm, seq)
    tf = inner // ML_HEADS
    nfb = inner // tf
    gpb = tf // MXU_DIM
    hb = tm // halo
    nhalo = m // halo
    head_dim = tf
    k_scale = head_dim ** -0.5
    assert 2.0 ** round(np.log2(k_scale)) == k_scale
    in_specs = [
        pl.BlockSpec((tm, tf), lambda i, j: (i, j)),
        pl.BlockSpec((halo, tf), lambda i, j: (jnp.maximum(i * hb - 1, 0), j)),
        pl.BlockSpec((halo, tf), lambda i, j: (jnp.minimum((i + 1) * hb, nhalo - 1), j)),
        pl.BlockSpec((ML_CONV_K, tf), lambda i, j: (0, j)),
        pl.BlockSpec((1, tf), lambda i, j: (0, j)),
        pl.BlockSpec((gpb, MXU_DIM, MXU_DIM), lambda i, j: (j, 0, 0)),
        pl.BlockSpec((gpb, MXU_DIM, MXU_DIM), lambda i, j: (j, 0, 0)),
        pl.BlockSpec((gpb, MXU_DIM, MXU_DIM), lambda i, j: (j, 0, 0)),
        pl.BlockSpec((3, tf, LANES), lambda i, j: (0, j, 0)),
        pl.BlockSpec((1, LANES), lambda i, j: (0, 0)),
    ]
    act = jax.ShapeDtypeStruct((m, inner), BF16)
    act_spec = pl.BlockSpec((tm, tf), lambda i, j: (i, j))
    blk = (5 * _nbytes((tm, tf), BF16) + 3 * _nbytes((gpb, MXU_DIM, MXU_DIM), BF16)
           + _nbytes((3, tf, LANES), BF16) + _nbytes((tm, LANES), F32))
    return pl.pallas_call(
        functools.partial(_ml_pre_kernel, seq=seq, k_scale=k_scale),
        grid=(m // tm, nfb),
        in_specs=in_specs,
        out_specs=[act_spec, act_spec, act_spec, act_spec, pl.BlockSpec((tm, LANES), lambda i, j: (i, 0))],
        out_shape=[act, act, act, act, jax.ShapeDtypeStruct((m, LANES), F32)],
        compiler_params=_params(2, blk, 0, 8 * _nbytes((tm, tf), F32)),
        name="mlstm_pre",
    )(up, up, up, conv_w, conv_b.reshape(1, inner), wq_bd, wk_bd, wv_bd, w_gates, b_gates)


def _split_dot(a, tri):
    hi = a.astype(BF16)
    r1 = a - hi.astype(F32)
    mid = r1.astype(BF16)
    lo = (r1 - mid.astype(F32)).astype(BF16)
    return (jnp.dot(hi, tri, preferred_element_type=F32) + jnp.dot(mid, tri, preferred_element_type=F32)
            + jnp.dot(lo, tri, preferred_element_type=F32))


def _gate_cumsum_kernel(f_ref, b_ref, *, n_chunks, n_dir_heads):
    f = f_ref[...]
    lf = jnp.minimum(f, 0.0) - jnp.log(1.0 + jnp.exp(-jnp.abs(f)))
    lc = f.shape[1]
    s = lax.broadcasted_iota(jnp.int32, (lc, lc), 0)
    l = lax.broadcasted_iota(jnp.int32, (lc, lc), 1)
    prefix = _split_dot(lf, (s <= l).astype(BF16))
    suffix = _split_dot(lf, (s >= l).astype(BF16))
    row = lax.broadcasted_iota(jnp.int32, (f.shape[0], 1), 0)
    backward = ((row // n_chunks) % (2 * n_dir_heads)) >= n_dir_heads
    b_ref[...] = jnp.where(backward, suffix, prefix)


def _gate_cumsum(f_rows, n_chunks):
    r, lc = f_rows.shape
    return pl.pallas_call(
        functools.partial(_gate_cumsum_kernel, n_chunks=n_chunks, n_dir_heads=ML_HEADS),
        grid=(1,),
        in_specs=[pl.BlockSpec((r, lc), lambda i: (0, 0))],
        out_specs=pl.BlockSpec((r, lc), lambda i: (0, 0)),
        out_shape=jax.ShapeDtypeStruct((r, lc), F32),
        compiler_params=_params(1, 2 * _nbytes((r, lc), F32), 0, 8 * _nbytes((r, lc), F32)),
        name="mlstm_gate_cumsum",
    )(f_rows)


def _ml_core_kernel(q_ref, k_ref, v_ref, ir_ref, br_ref, ic_ref, bc_ref, *rest, reverse, combine):
    if combine:
        hf_ref, xc_ref, z_ref, on_ref, sk_ref, h_ref, c_ref, n_ref, m_ref = rest
    else:
        h_ref, c_ref, n_ref, m_ref = rest
    hh = pl.program_id(1)
    step = pl.program_id(2)
    lc = q_ref.shape[0]

    @pl.when(step == 0)
    def _():
        c_ref[...] = jnp.zeros_like(c_ref)
        n_ref[...] = jnp.zeros_like(n_ref)
        m_ref[...] = jnp.zeros_like(m_ref)

    gate_col = hh + (ML_HEADS if reverse else 0)
    pick = lax.broadcasted_iota(jnp.int32, ic_ref.shape, 1) == gate_col
    i_col = jnp.sum(jnp.where(pick, ic_ref[...], 0.0), axis=1, keepdims=True)
    b_col = jnp.sum(jnp.where(pick, bc_ref[...], 0.0), axis=1, keepdims=True)
    i_row = ir_ref[0]
    b_row = br_ref[0]
    g = b_row[:, 0:1] if reverse else b_row[:, lc - 1:lc]
    m = m_ref[...]

    l_idx = lax.broadcasted_iota(jnp.int32, (lc, lc), 0)
    s_idx = lax.broadcasted_iota(jnp.int32, (lc, lc), 1)
    visible = (s_idx >= l_idx) if reverse else (s_idx <= l_idx)
    log_d = jnp.where(visible, b_col - b_row + i_row, -jnp.inf)
    m_inter = b_col + m
    m_row = jnp.maximum(m_inter, jnp.max(log_d, axis=1, keepdims=True))
    q = q_ref[...]
    k = k_ref[...]
    v = v_ref[...]
    s = lax.dot_general(q, k, (((1,), (1,)), ((), ())), preferred_element_type=F32) * jnp.exp(log_d - m_row)
    w_inter = jnp.exp(m_inter - m_row)
    q_c = jnp.dot(q, c_ref[...].astype(BF16), preferred_element_type=F32)
    q_n = jnp.sum(q.astype(F32) * n_ref[...], axis=1, keepdims=True)
    num = jnp.dot(s.astype(BF16), v, preferred_element_type=F32) + w_inter * q_c
    den = jnp.sum(s, axis=1, keepdims=True) + w_inter * q_n
    h = num / jnp.maximum(jnp.abs(den), jnp.exp(-m_row))
    if combine:
        hc = h + hf_ref[...].astype(F32)
        cen = hc - jnp.mean(hc, axis=-1, keepdims=True)
        var = jnp.mean(cen * cen, axis=-1, keepdims=True)
        hn = cen * lax.rsqrt(var + LN_EPS) * on_ref[...]
        h = (hn + sk_ref[...] * xc_ref[...].astype(F32)) * _silu(z_ref[...].astype(F32))
    h_ref[...] = h.astype(h_ref.dtype)

    log_w = g - b_col + i_col
    m_new = jnp.maximum(g + m, jnp.max(log_w, axis=0, keepdims=True))
    wk = k.astype(F32) * jnp.exp(log_w - m_new)
    decay = jnp.exp(g + m - m_new)
    upd = lax.dot_general(wk.astype(BF16), v, (((0,), (0,)), ((), ())), preferred_element_type=F32)
    c_ref[...] = decay * c_ref[...] + upd
    n_ref[...] = decay * n_ref[...] + jnp.sum(wk, axis=0, keepdims=True)
    m_ref[...] = m_new


def _ml_core(q, k, v, i_rows, b_rows, i_cols, b_cols, batch, seq, *, reverse, combine=None):
    m, inner = q.shape
    lc = ML_CHUNK
    nc = seq // lc
    dh = inner // ML_HEADS
    n_gate_cols = 2 * ML_HEADS

    def chunk(c):
        return nc - 1 - c if reverse else c

    def act_map(b, h, c):
        return (b * nc + chunk(c), h)

    def row_map(b, h, c):
        return ((b * n_gate_cols + h + (ML_HEADS if reverse else 0)) * nc + chunk(c), 0, 0)

    def col_map(b, h, c):
        return (b * nc + chunk(c), 0)

    act_spec = pl.BlockSpec((lc, dh), act_map)
    row_spec = pl.BlockSpec((1, 1, lc), row_map)
    col_spec = pl.BlockSpec((lc, LANES), col_map)
    blk = 4 * _nbytes((lc, dh), BF16) + 2 * _nbytes((8, lc), F32) + 2 * _nbytes((lc, LANES), F32)
    scratch = _nbytes((dh, dh), F32) + 2 * _nbytes((8, dh), F32)
    in_specs = [act_spec, act_spec, act_spec, row_spec, row_spec, col_spec, col_spec]
    args = [q, k, v, i_rows, b_rows, i_cols, b_cols]
    temp = 3 * _nbytes((dh, dh), F32)
    if combine is not None:
        h_other, x_c, up, out_norm, skip = combine
        z_spec = pl.BlockSpec((lc, dh), lambda b, h, c: (b * nc + chunk(c), ML_HEADS + h))
        vec_spec = pl.BlockSpec((1, dh), lambda b, h, c: (0, h))
        in_specs += [act_spec, act_spec, z_spec, vec_spec, vec_spec]
        args += [h_other, x_c, up, out_norm.reshape(1, inner), skip.reshape(1, inner)]
        blk += 3 * _nbytes((lc, dh), BF16)
        temp += 4 * _nbytes((lc, dh), F32)
    return pl.pallas_call(
        functools.partial(_ml_core_kernel, reverse=reverse, combine=combine is not None),
        grid=(batch, ML_HEADS, nc),
        in_specs=in_specs,
        out_specs=act_spec,
        out_shape=jax.ShapeDtypeStruct((m, inner), BF16),
        scratch_shapes=[pltpu.VMEM((dh, dh), F32), pltpu.VMEM((1, dh), F32), pltpu.VMEM((1, 1), F32)],
        compiler_params=_params(3, blk, scratch, temp),
        name="mlstm_core_bwd" if reverse else "mlstm_core_fwd",
    )(*args)


def _block_diag_tiles(w):
    per = MXU_DIM // ML_QKV_BLOCK
    w4 = w.reshape(-1, per, ML_QKV_BLOCK, ML_QKV_BLOCK)
    eye = jnp.eye(per, dtype=w.dtype)
    return jnp.einsum("ngio,gh->ngiho", w4, eye).reshape(-1, MXU_DIM, MXU_DIM).astype(BF16)


def _ml_layer(x, g, w_up, j, conv_w, conv_b, w_q, w_k, w_v, w_ig, b_ig, w_fg, b_fg, out_norm, skip, w_down,
              batch, seq):
    m = x.shape[0]
    inner = w_up.shape[2] // 2
    n_gate_cols = 2 * ML_HEADS
    nc = seq // ML_CHUNK
    up = _norm_matmul(x, g, w_up, j, tm=256)
    w_gates = jnp.concatenate([w_ig, w_fg], axis=1)
    w_gates = jnp.pad(w_gates, ((0, 0), (0, LANES - 2 * n_gate_cols))).reshape(3, inner, LANES).astype(BF16)
    b_gates = jnp.pad(jnp.concatenate([b_ig, b_fg]), (0, LANES - 2 * n_gate_cols)).reshape(1, LANES).astype(F32)
    q, k, v, x_c, gates = _ml_pre(up, conv_w.astype(F32), conv_b.astype(F32), _block_diag_tiles(w_q),
                                  _block_diag_tiles(w_k), _block_diag_tiles(w_v), w_gates, b_gates, seq)

    def to_rows(cols):
        return cols.reshape(batch, nc, ML_CHUNK, n_gate_cols).transpose(0, 3, 1, 2).reshape(-1, ML_CHUNK)

    def to_cols(rows):
        cols = rows.reshape(batch, n_gate_cols, nc, ML_CHUNK).transpose(0, 2, 3, 1).reshape(m, n_gate_cols)
        return jnp.pad(cols, ((0, 0), (0, LANES - n_gate_cols)))

    i_rows = to_rows(gates[:, :n_gate_cols])
    b_rows = _gate_cumsum(to_rows(gates[:, n_gate_cols:2 * n_gate_cols]), nc)
    b_cols = to_cols(b_rows)
    i_rows3 = i_rows.reshape(-1, 1, ML_CHUNK)
    b_rows3 = b_rows.reshape(-1, 1, ML_CHUNK)
    h_f = _ml_core(q, k, v, i_rows3, b_rows3, gates, b_cols, batch, seq, reverse=False)
    y = _ml_core(q, k, v, i_rows3, b_rows3, gates, b_cols, batch, seq, reverse=True,
                 combine=(h_f, x_c, up, out_norm.astype(F32), skip.astype(F32)))
    return _matmul_res(y, w_down, j, x)


def kernel(x, norm_mix, norm_mlp, na_w_qkv, na_q_gain, na_k_gain, na_rel_bias, na_w_o, ml_w_up, ml_conv_w, ml_conv_b, ml_w_q, ml_w_k, ml_w_v, ml_w_ig, ml_b_ig, ml_w_fg, ml_b_fg, ml_out_norm, ml_skip, ml_w_down, mlp_w1, mlp_w2):
    batch, seq, d = x.shape
    depth = norm_mix.shape[0]
    n_mixers = 2
    na_w_qkv, na_w_o, ml_w_up, ml_w_down, mlp_w1, mlp_w2 = (
        w.astype(BF16) for w in (na_w_qkv, na_w_o, ml_w_up, ml_w_down, mlp_w1, mlp_w2))
    h = x.reshape(batch * seq, d).astype(F32)
    for layer in range(depth):
        j = layer // n_mixers
        if layer % n_mixers == 0:
            h = _na_layer(h, norm_mix[layer], na_w_qkv, j, na_q_gain[j], na_k_gain[j], na_rel_bias[j],
                          na_w_o, batch, seq)
        else:
            h = _ml_layer(h, norm_mix[layer], ml_w_up, j, ml_conv_w[j], ml_conv_b[j], ml_w_q[j], ml_w_k[j],
                          ml_w_v[j], ml_w_ig[j], ml_b_ig[j], ml_w_fg[j], ml_b_fg[j], ml_out_norm[j],
                          ml_skip[j], ml_w_down, batch, seq)
        h = _mlp(h, norm_mlp[layer], mlp_w1, mlp_w2, layer)
    return h.reshape(batch, seq, d).astype(x.dtype)
```
